```python
import jax, jax.numpy as jnp
from jax import lax
import numpy as np

D_MODEL = 2048
BATCH = 2
SEQ = 16384
DEPTH = 1

RET_HEADS = 8
RET_QK_DIM = 128
RET_V_DIM = 128
RET_QK_WIDTH = RET_HEADS * RET_QK_DIM
RET_WIDTH = RET_HEADS * RET_V_DIM
CONV_WIDTH = D_MODEL // 2
CONV_GROUPS = 8
CONV_K = 3
CHUNK = 128
N_BRANCH = 2
ROPE_BASE = 10000.0
EPS = 1e-6
COL_SIZES = (RET_QK_WIDTH, RET_QK_WIDTH, RET_WIDTH, RET_WIDTH,
             CONV_WIDTH, CONV_WIDTH, CONV_WIDTH, CONV_WIDTH,
             N_BRANCH * D_MODEL)
COL_SPLITS = tuple(int(c) for c in np.cumsum(COL_SIZES)[:-1])
IN_COLS = int(sum(COL_SIZES))
BRANCH_WIDTH = RET_WIDTH

kernel_name = "hybrid_retention_shortconv_gated_block"


def rmsnorm(x, gain):
    xf = x.astype(jnp.float32)
    y = xf * lax.rsqrt(jnp.mean(xf * xf, axis=-1, keepdims=True) + EPS)
    return (y * gain.astype(jnp.float32)).astype(x.dtype)


def rotary(x):
    s, dh = x.shape[1], x.shape[-1]
    pos = jnp.arange(s, dtype=jnp.float32)
    inv_freq = ROPE_BASE ** (-jnp.arange(0, dh, 2, dtype=jnp.float32) / dh)
    ang = pos[:, None] * inv_freq[None, :]
    cos = jnp.cos(ang)[None, :, None, :]
    sin = jnp.sin(ang)[None, :, None, :]
    xf = x.astype(jnp.float32)
    x1, x2 = jnp.split(xf, 2, axis=-1)
    return jnp.concatenate([x1 * cos - x2 * sin, x2 * cos + x1 * sin], axis=-1)


def retention_dir(q, k, v, log_gamma, strict):
    b, h, s, dk = q.shape
    dv = v.shape[-1]
    n = s // CHUNK
    qc = q.reshape(b, h, n, CHUNK, dk)
    kc = k.reshape(b, h, n, CHUNK, dk)
    vc = v.reshape(b, h, n, CHUNK, dv)
    pos = jnp.arange(CHUNK, dtype=jnp.float32)
    diff = pos[:, None] - pos[None, :]
    mask = (diff > 0) if strict else (diff >= 0)
    lg = log_gamma[:, None, None]
    decay = jnp.where(mask[None], jnp.exp(jnp.where(mask, diff, 0.0)[None] * lg), 0.0)
    scores = jnp.einsum('bhncd,bhnld->bhncl', qc, kc) * decay[None, :, None]
    intra = jnp.einsum('bhncl,bhnle->bhnce', scores, vc)
    k_w = jnp.exp((CHUNK - pos)[None, :] * log_gamma[:, None])
    kv = jnp.einsum('bhncd,hc,bhnce->bhnde', kc, k_w, vc)
    chunk_decay = jnp.exp(CHUNK * log_gamma)[None, :, None, None]

    def step(state, kv_n):
        return chunk_decay * state + kv_n, state

    _, states = lax.scan(step, jnp.zeros((b, h, dk, dv), jnp.float32), jnp.moveaxis(kv, 2, 0))
    states = jnp.moveaxis(states, 0, 2)
    q_w = jnp.exp(pos[None, :] * log_gamma[:, None])
    cross = jnp.einsum('bhncd,hc,bhnde->bhnce', qc, q_w, states)
    return (intra + cross).reshape(b, h, s, dv)


def bidirectional_retention(q, k, v, logit_fwd, logit_bwd):
    lg_f = jax.nn.log_sigmoid(logit_fwd.astype(jnp.float32))
    lg_b = jax.nn.log_sigmoid(logit_bwd.astype(jnp.float32))
    fwd = retention_dir(q, k, v, lg_f, strict=False)
    flip = lambda t: jnp.flip(t, axis=2)
    bwd = flip(retention_dir(flip(q), flip(k), flip(v), lg_b, strict=True))
    return fwd + bwd


def short_conv_centred(u, w):
    rhs = w[:, None, :].astype(u.dtype)
    return lax.conv_general_dilated(
        u, rhs, window_strides=(1,), padding=[(CONV_K // 2, CONV_K // 2)],
        dimension_numbers=('NWC', 'WIO', 'NWC'), feature_group_count=u.shape[-1])


def setup_inputs(seed: int = 0) -> dict:
    key = jax.random.key(seed)
    ks = jax.random.split(key, 12)
    f32 = jnp.float32
    x = jax.random.normal(ks[0], (BATCH, SEQ, D_MODEL), f32)
    norm_gain = 1.0 + 0.02 * jax.random.normal(ks[1], (DEPTH, D_MODEL), f32)
    w_in = jax.random.normal(ks[2], (DEPTH, D_MODEL, IN_COLS), f32) * D_MODEL ** -0.5
    base_logit = jnp.log(2.0 ** (5.0 + jnp.arange(RET_HEADS, dtype=f32)) - 1.0)
    decay_logit_fwd = base_logit[None] + 0.1 * jax.random.normal(ks[3], (DEPTH, RET_HEADS), f32)
    decay_logit_bwd = base_logit[None] + 0.1 * jax.random.normal(ks[4], (DEPTH, RET_HEADS), f32)
    ret_gn_gain = 1.0 + 0.02 * jax.random.normal(ks[5], (DEPTH, RET_WIDTH), f32)
    conv_w = jax.random.normal(ks[6], (DEPTH, CONV_K, CONV_WIDTH), f32) * CONV_K ** -0.5
    w_branch = jax.random.normal(ks[7], (DEPTH, N_BRANCH, BRANCH_WIDTH, D_MODEL), f32) * BRANCH_WIDTH ** -0.5
    w_out = jax.random.normal(ks[8], (DEPTH, D_MODEL, D_MODEL), f32) * D_MODEL ** -0.5
    final_gain = 1.0 + 0.02 * jax.random.normal(ks[9], (D_MODEL,), f32)
    return {"x": x, "norm_gain": norm_gain, "w_in": w_in,
            "decay_logit_fwd": decay_logit_fwd, "decay_logit_bwd": decay_logit_bwd,
            "ret_gn_gain": ret_gn_gain, "conv_w": conv_w, "w_branch": w_branch,
            "w_out": w_out, "final_gain": final_gain}


def reference(x, norm_gain, w_in, decay_logit_fwd, decay_logit_bwd, ret_gn_gain,
              conv_w, w_branch, w_out, final_gain):
    b, s, _ = x.shape
    for layer in range(DEPTH):
        h = rmsnorm(x, norm_gain[layer])
        proj = jnp.einsum('bsd,df->bsf', h, w_in[layer])
        q, k, v, g_ret, c_b, c_c, c_x, g_conv, merge_logits = jnp.split(proj, COL_SPLITS, axis=-1)

        q = rotary(q.reshape(b, s, RET_HEADS, RET_QK_DIM))
        k = rotary(k.reshape(b, s, RET_HEADS, RET_QK_DIM)) * (RET_QK_DIM ** -0.5)
        v = v.reshape(b, s, RET_HEADS, RET_V_DIM).astype(jnp.float32)
        to_bhsd = lambda t: jnp.transpose(t, (0, 2, 1, 3))
        o = bidirectional_retention(to_bhsd(q), to_bhsd(k), to_bhsd(v),
                                    decay_logit_fwd[layer], decay_logit_bwd[layer])
        o = jnp.transpose(o, (0, 2, 1, 3))
        o = o * lax.rsqrt(jnp.mean(o * o, axis=-1, keepdims=True) + EPS)
        o = o.reshape(b, s, RET_WIDTH) * ret_gn_gain[layer].astype(jnp.float32)
        branch_ret = (o * jax.nn.silu(g_ret.astype(jnp.float32))).astype(x.dtype)

        conv = short_conv_centred(c_c * c_x, conv_w[layer])
        branch_conv = c_b * conv * jax.nn.silu(g_conv)

        branches = jnp.stack([branch_ret, branch_conv], axis=2)
        up = jnp.einsum('bsnf,nfd->bsnd', branches, w_branch[layer])
        gates = jax.nn.sigmoid(merge_logits.reshape(b, s, N_BRANCH, D_MODEL))
        merged = jnp.sum(gates * up, axis=2)
        out = jnp.einsum('bsd,de->bse', merged, w_out[layer])
        x = x + out.astype(x.dtype)
    return rmsnorm(x, final_gain)
```

```python
import functools

import jax
import jax.numpy as jnp
from jax import lax
from jax.experimental import pallas as pl
from jax.experimental.pallas import tpu as pltpu

F32 = jnp.float32
BF16 = jnp.bfloat16

D_MODEL = 2048
HEADS = 8
HEAD_DIM = 128
WIDTH = HEADS * HEAD_DIM
CHUNK = 128
CONV_K = 3
ROPE_BASE = 10000.0
EPS = 1e-6
G_Q, G_K, G_V, G_GRET, G_CB, G_CC, G_CX, G_GCONV, G_GATE0 = range(9)
N_COL_TILES = 12
N_HEAD_GROUPS = 4
N_ROW_GROUPS = 7

PROJ_ROWS = 1024
MERGE_ROWS = 256
HALO_ROWS = 16
VMEM_LIMIT = 56 * 1024 * 1024


def _proj_kernel(x_ref, gain_ref, w_ref, cos_ref, sin_ref, heads_ref, rows_ref, h_ref, cc_ref):
    j = pl.program_id(1)

    @pl.when(j == 0)
    def _():
        x = x_ref[...]
        ms = jnp.mean(x * x, axis=-1, keepdims=True)
        h_ref[...] = (x * lax.rsqrt(ms + EPS) * gain_ref[...]).astype(BF16)

    def project():
        return jnp.dot(h_ref[...], w_ref[...], preferred_element_type=F32)

    def store_heads(r):
        for hd in range(HEADS):
            heads_ref[hd] = r[:, hd * HEAD_DIM:(hd + 1) * HEAD_DIM].astype(BF16)

    def store_rotary(r, scale):
        cos = cos_ref[...]
        sin = sin_ref[...]
        for hd in range(HEADS):
            rh = r[:, hd * HEAD_DIM:(hd + 1) * HEAD_DIM]
            o = rh * cos + pltpu.roll(rh, HEAD_DIM // 2, axis=1) * sin
            if scale is not None:
                o = o * scale
            heads_ref[hd] = o.astype(BF16)

    @pl.when(j == G_Q)
    def _():
        store_rotary(project(), None)

    @pl.when(j == G_K)
    def _():
        store_rotary(project(), HEAD_DIM ** -0.5)

    @pl.when(j == G_V)
    def _():
        store_heads(project())

    @pl.when(j == G_GRET)
    def _():
        r = project()
        store_heads(r * jax.nn.sigmoid(r))

    @pl.when(j == G_CB)
    def _():
        rows_ref[0] = project().astype(BF16)

    @pl.when(j == G_CC)
    def _():
        cc_ref[...] = project().astype(BF16)

    @pl.when(j == G_CX)
    def _():
        rows_ref[0] = (cc_ref[...].astype(F32) * project()).astype(BF16)

    @pl.when(j == G_GCONV)
    def _():
        r = project()
        rows_ref[0] = (r * jax.nn.sigmoid(r)).astype(BF16)

    @pl.when(j >= G_GATE0)
    def _():
        rows_ref[0] = jax.nn.sigmoid(project()).astype(BF16)


def _project(x2, gain, w_bf, cos, sin, seq):
    tokens = x2.shape[0]
    tm = PROJ_ROWS
    seq_tiles = seq // tm
    grid = (tokens // tm, N_COL_TILES)
    return pl.pallas_call(
        _proj_kernel,
        grid=grid,
        in_specs=[
            pl.BlockSpec((tm, D_MODEL), lambda i, j: (i, 0)),
            pl.BlockSpec((1, D_MODEL), lambda i, j: (0, 0)),
            pl.BlockSpec((D_MODEL, WIDTH), lambda i, j: (0, j)),
            pl.BlockSpec((tm, HEAD_DIM), lambda i, j: (i % seq_tiles, 0)),
            pl.BlockSpec((tm, HEAD_DIM), lambda i, j: (i % seq_tiles, 0)),
        ],
        out_specs=[
            pl.BlockSpec((HEADS, tm, HEAD_DIM),
                         lambda i, j: (jnp.minimum(j, N_HEAD_GROUPS - 1), i, 0)),
            pl.BlockSpec((1, tm, WIDTH),
                         lambda i, j: (jnp.where(j <= G_CC, 0, j - G_CC), i, 0)),
        ],
        out_shape=[
            jax.ShapeDtypeStruct((N_HEAD_GROUPS * HEADS, tokens, HEAD_DIM), BF16),
            jax.ShapeDtypeStruct((N_ROW_GROUPS, tokens, WIDTH), BF16),
        ],
        scratch_shapes=[
            pltpu.VMEM((tm, D_MODEL), BF16),
            pltpu.VMEM((tm, WIDTH), BF16),
        ],
        compiler_params=pltpu.CompilerParams(
            dimension_semantics=("arbitrary", "arbitrary"),
            vmem_limit_bytes=VMEM_LIMIT),
        name="proj",
    )(x2, gain, w_bf, cos, sin)


def _log_sigmoid(z):
    return -(jnp.maximum(-z, 0.0) + jnp.log1p(jnp.exp(-jnp.abs(z))))


def _ret_kernel(logit_ref, q_ref, k_ref, v_ref, g_ref, gain_ref, o_ref, sb_ref, *, n_chunks):
    lf = _log_sigmoid(logit_ref[0, 0])[0:1, :]
    lb = _log_sigmoid(logit_ref[0, 1])[0:1, :]
    row = lax.broadcasted_iota(jnp.int32, (CHUNK, CHUNK), 0).astype(F32)
    col = lax.broadcasted_iota(jnp.int32, (CHUNK, CHUNK), 1).astype(F32)
    dist = row - col
    decay = jnp.exp(jnp.where(dist >= 0, dist * lf, -dist * lb))
    wq_f = jnp.exp(row * lf)
    wq_b = jnp.exp((CHUNK - 1 - row) * lb)
    wk_f = jnp.exp((CHUNK - row) * lf)
    wk_b = jnp.exp((row + 1) * lb)
    chunk_f = jnp.exp(CHUNK * lf)
    chunk_b = jnp.exp(CHUNK * lb)
    gain = gain_ref[0]

    def rows_of(c):
        return pl.ds(pl.multiple_of(c * CHUNK, CHUNK), CHUNK)

    def kv_summary(k_weighted, v):
        return lax.dot_general(k_weighted, v, (((0,), (0,)), ((), ())),
                               preferred_element_type=F32)

    def bwd_body(i, state):
        c = n_chunks - 1 - i
        sb_ref[c] = state.astype(BF16)
        rows = rows_of(c)
        kb = (k_ref[0, rows, :].astype(F32) * wk_b).astype(BF16)
        return chunk_b * state + kv_summary(kb, v_ref[0, rows, :])

    lax.fori_loop(0, n_chunks, bwd_body, jnp.zeros((CHUNK, HEAD_DIM), F32))

    def fwd_body(c, state):
        rows = rows_of(c)
        q = q_ref[0, rows, :]
        k = k_ref[0, rows, :]
        v = v_ref[0, rows, :]
        scores = lax.dot_general(q, k, (((1,), (1,)), ((), ())), preferred_element_type=F32)
        qf32 = q.astype(F32)
        lhs = jnp.concatenate([(scores * decay).astype(BF16),
                               (qf32 * wq_f).astype(BF16),
                               (qf32 * wq_b).astype(BF16)], axis=1)
        rhs = jnp.concatenate([v, state.astype(BF16), sb_ref[c]], axis=0)
        o = jnp.dot(lhs, rhs, preferred_element_type=F32)
        ms = jnp.mean(o * o, axis=-1, keepdims=True)
        o = o * lax.rsqrt(ms + EPS) * gain * g_ref[0, rows, :].astype(F32)
        o_ref[0, rows, :] = o.astype(BF16)
        kf = (k.astype(F32) * wk_f).astype(BF16)
        return chunk_f * state + kv_summary(kf, v)

    lax.fori_loop(0, n_chunks, fwd_body, jnp.zeros((CHUNK, HEAD_DIM), F32))


def _retention(heads, logits, gn_gain, batch, seq):
    tokens = batch * seq
    n_chunks = seq // CHUNK

    def head_spec(group):
        return pl.BlockSpec((1, seq, HEAD_DIM),
                            lambda p: (group * HEADS + p % HEADS, p // HEADS, 0))

    return pl.pallas_call(
        functools.partial(_ret_kernel, n_chunks=n_chunks),
        grid=(batch * HEADS,),
        in_specs=[
            pl.BlockSpec((1, 2, 8, HEAD_DIM), lambda p: (p % HEADS, 0, 0, 0)),
            head_spec(G_Q), head_spec(G_K), head_spec(G_V), head_spec(G_GRET),
            pl.BlockSpec((1, 1, HEAD_DIM), lambda p: (p % HEADS, 0, 0)),
        ],
        out_specs=pl.BlockSpec((1, seq, HEAD_DIM), lambda p: (p % HEADS, p // HEADS, 0)),
        out_shape=jax.ShapeDtypeStruct((HEADS, tokens, HEAD_DIM), BF16),
        scratch_shapes=[pltpu.VMEM((n_chunks, CHUNK, HEAD_DIM), BF16)],
        compiler_params=pltpu.CompilerParams(
            dimension_semantics=("arbitrary",),
            vmem_limit_bytes=VMEM_LIMIT),
        name="retention",
    )(logits, heads, heads, heads, heads, gn_gain)


def _merge_kernel(rows_ref, uprev_ref, unext_ref, ret_ref, x_ref, convw_ref, wb_ref, wo_ref,
                  fgain_ref, out_ref, *, seq_tiles):
    tm = x_ref.shape[0]
    i = pl.program_id(0)
    pos = i % seq_tiles
    u = rows_ref[1].astype(F32)
    prev_row = jnp.where(pos == 0, 0.0, uprev_ref[0, HALO_ROWS - 1:HALO_ROWS, :].astype(F32))
    next_row = jnp.where(pos == seq_tiles - 1, 0.0, unext_ref[0, 0:1, :].astype(F32))
    ridx = lax.broadcasted_iota(jnp.int32, (tm, WIDTH), 0)
    u_prev = jnp.where(ridx == 0, prev_row, pltpu.roll(u, 1, axis=0))
    u_next = jnp.where(ridx == tm - 1, next_row, pltpu.roll(u, tm - 1, axis=0))
    w = convw_ref[...]
    conv = w[0:1, :] * u_prev + w[1:2, :] * u + w[2:3, :] * u_next
    branch_conv = (rows_ref[0].astype(F32) * conv * rows_ref[2].astype(F32)).astype(BF16)
    branch_ret = jnp.concatenate([ret_ref[hd] for hd in range(HEADS)], axis=1)

    halves = []
    for half in range(D_MODEL // WIDTH):
        cols = slice(half * WIDTH, (half + 1) * WIDTH)
        up_ret = jnp.dot(branch_ret, wb_ref[0, :, cols], preferred_element_type=F32)
        up_conv = jnp.dot(branch_conv, wb_ref[1, :, cols], preferred_element_type=F32)
        gate_ret = rows_ref[3 + half].astype(F32)
        gate_conv = rows_ref[3 + D_MODEL // WIDTH + half].astype(F32)
        halves.append((gate_ret * up_ret + gate_conv * up_conv).astype(BF16))
    merged = jnp.concatenate(halves, axis=1)
    y = x_ref[...] + jnp.dot(merged, wo_ref[...], preferred_element_type=F32)
    ms = jnp.mean(y * y, axis=-1, keepdims=True)
    out_ref[...] = y * lax.rsqrt(ms + EPS) * fgain_ref[...]


def _merge(rows, ret, x2, conv_w, wb_bf, wo_bf, fgain, seq):
    tokens = x2.shape[0]
    tm = MERGE_ROWS
    seq_tiles = seq // tm
    halo_per_tile = tm // HALO_ROWS
    n_halo = tokens // HALO_ROWS
    resident = pl.Buffered(1)
    return pl.pallas_call(
        functools.partial(_merge_kernel, seq_tiles=seq_tiles),
        grid=(tokens // tm,),
        in_specs=[
            pl.BlockSpec((N_ROW_GROUPS, tm, WIDTH), lambda i: (0, i, 0)),
            pl.BlockSpec((1, HALO_ROWS, WIDTH),
                         lambda i: (1, jnp.maximum(i * halo_per_tile - 1, 0), 0)),
            pl.BlockSpec((1, HALO_ROWS, WIDTH),
                         lambda i: (1, jnp.minimum((i + 1) * halo_per_tile, n_halo - 1), 0)),
            pl.BlockSpec((HEADS, tm, HEAD_DIM), lambda i: (0, i, 0)),
            pl.BlockSpec((tm, D_MODEL), lambda i: (i, 0)),
            pl.BlockSpec((CONV_K, WIDTH), lambda i: (0, 0)),
            pl.BlockSpec((2, WIDTH, D_MODEL), lambda i: (0, 0, 0), pipeline_mode=resident),
            pl.BlockSpec((D_MODEL, D_MODEL), lambda i: (0, 0), pipeline_mode=resident),
            pl.BlockSpec((1, D_MODEL), lambda i: (0, 0)),
        ],
        out_specs=pl.BlockSpec((tm, D_MODEL), lambda i: (i, 0)),
        out_shape=jax.ShapeDtypeStruct((tokens, D_MODEL), F32),
        compiler_params=pltpu.CompilerParams(
            dimension_semantics=("arbitrary",),
            vmem_limit_bytes=VMEM_LIMIT),
        name="merge",
    )(rows, rows, rows, ret, x2, conv_w, wb_bf, wo_bf, fgain)


def _rotary_tables(seq):
    pos = jnp.arange(seq, dtype=F32)
    inv_freq = ROPE_BASE ** (-jnp.arange(0, HEAD_DIM, 2, dtype=F32) / HEAD_DIM)
    ang = pos[:, None] * inv_freq[None, :]
    cos = jnp.cos(ang)
    sin = jnp.sin(ang)
    return jnp.concatenate([cos, cos], axis=1), jnp.concatenate([-sin, sin], axis=1)


def kernel(x, norm_gain, w_in, decay_logit_fwd, decay_logit_bwd, ret_gn_gain, conv_w, w_branch,
           w_out, final_gain):
    batch, seq, d_model = x.shape
    assert d_model == D_MODEL and norm_gain.shape[0] == 1
    assert w_in.shape == (1, D_MODEL, N_COL_TILES * WIDTH)
    assert seq % PROJ_ROWS == 0 and seq % MERGE_ROWS == 0 and seq % CHUNK == 0
    tokens = batch * seq
    x2 = x.reshape(tokens, D_MODEL)
    cos, sin = _rotary_tables(seq)
    heads, rows = _project(x2, norm_gain[0][None, :], w_in[0].astype(BF16), cos, sin, seq)

    logits = jnp.stack([decay_logit_fwd[0], decay_logit_bwd[0]], axis=1).astype(F32)
    logits = jnp.broadcast_to(logits[:, :, None, None], (HEADS, 2, 8, HEAD_DIM))
    gn_gain = ret_gn_gain[0].astype(F32).reshape(HEADS, 1, HEAD_DIM)
    ret = _retention(heads, logits, gn_gain, batch, seq)

    out = _merge(rows, ret, x2, conv_w[0], w_branch[0].astype(BF16), w_out[0].astype(BF16),
                 final_gain[None, :], seq)
    return out.reshape(batch, seq, D_MODEL)
```

```python
import functools

import jax
import jax.numpy as jnp
from jax import lax
from jax.experimental import pallas as pl
from jax.experimental.pallas import tpu as pltpu

F32 = jnp.float32
BF16 = jnp.bfloat16

D_MODEL = 2048
HEADS = 8
HEAD_DIM = 128
WIDTH = HEADS * HEAD_DIM
CHUNK = 128
CONV_K = 3
ROPE_BASE = 10000.0
EPS = 1e-6
G_Q, G_K, G_V, G_GRET, G_CB, G_CC, G_CX, G_GCONV, G_GATE0 = range(9)
N_COL_TILES = 12
N_HEAD_GROUPS = 4
N_ROW_GROUPS = 7

PROJ_ROWS = 1024
PROJ_SUB_ROWS = 256
MERGE_ROWS = 512
MERGE_SUB_ROWS = 256
STATE_UNROLL = 8
OUT_UNROLL = 8
HALO_ROWS = 16
VMEM_LIMIT = 56 * 1024 * 1024


def _proj_kernel(x_ref, gain_ref, w_ref, cos_ref, sin_ref, heads_ref, rows_ref, h_ref, cc_ref):
    j = pl.program_id(1)

    def row_blocks(normalize=False):
        for r0 in range(0, h_ref.shape[0], PROJ_SUB_ROWS):
            rs = slice(r0, r0 + PROJ_SUB_ROWS)
            if normalize:
                x = x_ref[rs, :]
                ms = jnp.mean(x * x, axis=-1, keepdims=True)
                h = (x * lax.rsqrt(ms + EPS) * gain_ref[...]).astype(BF16)
                h_ref[rs, :] = h
            else:
                h = h_ref[rs, :]
            yield rs, jnp.dot(h, w_ref[...], preferred_element_type=F32)

    def store_heads(rs, r):
        for hd in range(HEADS):
            heads_ref[hd, rs, :] = r[:, hd * HEAD_DIM:(hd + 1) * HEAD_DIM].astype(BF16)

    def store_rotary(rs, r, scale):
        cos = cos_ref[rs, :]
        sin = sin_ref[rs, :]
        for hd in range(HEADS):
            rh = r[:, hd * HEAD_DIM:(hd + 1) * HEAD_DIM]
            o = rh * cos + pltpu.roll(rh, HEAD_DIM // 2, axis=1) * sin
            if scale is not None:
                o = o * scale
            heads_ref[hd, rs, :] = o.astype(BF16)

    @pl.when(j == G_Q)
    def _():
        for rs, r in row_blocks(normalize=True):
            store_rotary(rs, r, None)

    @pl.when(j == G_K)
    def _():
        for rs, r in row_blocks():
            store_rotary(rs, r, HEAD_DIM ** -0.5)

    @pl.when(j == G_V)
    def _():
        for rs, r in row_blocks():
            store_heads(rs, r)

    @pl.when(j == G_GRET)
    def _():
        for rs, r in row_blocks():
            store_heads(rs, r * jax.nn.sigmoid(r))

    @pl.when(j == G_CB)
    def _():
        for rs, r in row_blocks():
            rows_ref[0, rs, :] = r.astype(BF16)

    @pl.when(j == G_CC)
    def _():
        for rs, r in row_blocks():
            cc_ref[rs, :] = r.astype(BF16)

    @pl.when(j == G_CX)
    def _():
        for rs, r in row_blocks():
            rows_ref[0, rs, :] = (cc_ref[rs, :].astype(F32) * r).astype(BF16)

    @pl.when(j == G_GCONV)
    def _():
        for rs, r in row_blocks():
            rows_ref[0, rs, :] = (r * jax.nn.sigmoid(r)).astype(BF16)

    @pl.when(j >= G_GATE0)
    def _():
        for rs, r in row_blocks():
            rows_ref[0, rs, :] = jax.nn.sigmoid(r).astype(BF16)


def _project(x2, gain, w_bf, cos, sin, seq):
    tokens = x2.shape[0]
    tm = PROJ_ROWS
    seq_tiles = seq // tm
    grid = (tokens // tm, N_COL_TILES)
    return pl.pallas_call(
        _proj_kernel,
        grid=grid,
        in_specs=[
            pl.BlockSpec((tm, D_MODEL), lambda i, j: (i, 0)),
            pl.BlockSpec((1, D_MODEL), lambda i, j: (0, 0)),
            pl.BlockSpec((D_MODEL, WIDTH), lambda i, j: (0, j)),
            pl.BlockSpec((tm, HEAD_DIM), lambda i, j: (i % seq_tiles, 0)),
            pl.BlockSpec((tm, HEAD_DIM), lambda i, j: (i % seq_tiles, 0)),
        ],
        out_specs=[
            pl.BlockSpec((HEADS, tm, HEAD_DIM),
                         lambda i, j: (jnp.minimum(j, N_HEAD_GROUPS - 1), i, 0)),
            pl.BlockSpec((1, tm, WIDTH),
                         lambda i, j: (jnp.where(j <= G_CC, 0, j - G_CC), i, 0)),
        ],
        out_shape=[
            jax.ShapeDtypeStruct((N_HEAD_GROUPS * HEADS, tokens, HEAD_DIM), BF16),
            jax.ShapeDtypeStruct((N_ROW_GROUPS, tokens, WIDTH), BF16),
        ],
        scratch_shapes=[
            pltpu.VMEM((tm, D_MODEL), BF16),
            pltpu.VMEM((tm, WIDTH), BF16),
        ],
        compiler_params=pltpu.CompilerParams(
            dimension_semantics=("arbitrary", "arbitrary"),
            vmem_limit_bytes=VMEM_LIMIT),
        name="proj",
    )(x2, gain, w_bf, cos, sin)


def _log_sigmoid(z):
    return -(jnp.maximum(-z, 0.0) + jnp.log1p(jnp.exp(-jnp.abs(z))))


def _ret_kernel(logit_ref, q_ref, k_ref, v_ref, g_ref, gain_ref, o_ref, sf_ref, sb_ref, p_ref, *,
                n_chunks):
    lf = _log_sigmoid(logit_ref[0, 0])[0:1, :]
    lb = _log_sigmoid(logit_ref[0, 1])[0:1, :]
    row = lax.broadcasted_iota(jnp.int32, (CHUNK, CHUNK), 0).astype(F32)
    col = lax.broadcasted_iota(jnp.int32, (CHUNK, CHUNK), 1).astype(F32)
    dist = row - col
    decay = jnp.exp(jnp.where(dist >= 0, dist * lf, -dist * lb))
    wq_f = jnp.exp(row * lf)
    wq_b = jnp.exp((CHUNK - 1 - row) * lb)
    wk_f = jnp.exp((CHUNK - row) * lf)
    wk_b = jnp.exp((row + 1) * lb)
    chunk_f = jnp.exp(CHUNK * lf)
    chunk_b = jnp.exp(CHUNK * lb)
    gain = gain_ref[0]

    def rows_of(c):
        return pl.ds(pl.multiple_of(c * CHUNK, CHUNK), CHUNK)

    def kv_summary(k_weighted, v):
        return lax.dot_general(k_weighted, v, (((0,), (0,)), ((), ())),
                               preferred_element_type=F32)

    def state_body(i, carry):
        state_f, state_b = carry
        cf = i
        cb = n_chunks - 1 - i
        sf_ref[cf] = state_f.astype(BF16)
        sb_ref[cb] = state_b.astype(BF16)
        rows_f = rows_of(cf)
        rows_b = rows_of(cb)
        k_f = k_ref[0, rows_f, :]
        scores = lax.dot_general(q_ref[0, rows_f, :], k_f, (((1,), (1,)), ((), ())),
                                 preferred_element_type=F32)
        p_ref[rows_f, :] = (scores * decay).astype(BF16)
        kf = (k_f.astype(F32) * wk_f).astype(BF16)
        kb = (k_ref[0, rows_b, :].astype(F32) * wk_b).astype(BF16)
        return (chunk_f * state_f + kv_summary(kf, v_ref[0, rows_f, :]),
                chunk_b * state_b + kv_summary(kb, v_ref[0, rows_b, :]))

    zero = jnp.zeros((CHUNK, HEAD_DIM), F32)
    lax.fori_loop(0, n_chunks, state_body, (zero, zero), unroll=STATE_UNROLL)

    def out_body(c, carry):
        rows = rows_of(c)
        qf32 = q_ref[0, rows, :].astype(F32)
        lhs = jnp.concatenate([p_ref[rows, :],
                               (qf32 * wq_f).astype(BF16),
                               (qf32 * wq_b).astype(BF16)], axis=1)
        rhs = jnp.concatenate([v_ref[0, rows, :], sf_ref[c], sb_ref[c]], axis=0)
        o = jnp.dot(lhs, rhs, preferred_element_type=F32)
        ms = jnp.mean(o * o, axis=-1, keepdims=True)
        o = o * lax.rsqrt(ms + EPS) * gain * g_ref[0, rows, :].astype(F32)
        o_ref[0, rows, :] = o.astype(BF16)
        return carry

    lax.fori_loop(0, n_chunks, out_body, 0, unroll=OUT_UNROLL)


def _retention(heads, logits, gn_gain, batch, seq):
    tokens = batch * seq
    n_chunks = seq // CHUNK

    def head_spec(group):
        return pl.BlockSpec((1, seq, HEAD_DIM),
                            lambda p: (group * HEADS + p % HEADS, p // HEADS, 0))

    return pl.pallas_call(
        functools.partial(_ret_kernel, n_chunks=n_chunks),
        grid=(batch * HEADS,),
        in_specs=[
            pl.BlockSpec((1, 2, 8, HEAD_DIM), lambda p: (p % HEADS, 0, 0, 0)),
            head_spec(G_Q), head_spec(G_K), head_spec(G_V), head_spec(G_GRET),
            pl.BlockSpec((1, 1, HEAD_DIM), lambda p: (p % HEADS, 0, 0)),
        ],
        out_specs=pl.BlockSpec((1, seq, HEAD_DIM), lambda p: (p % HEADS, p // HEADS, 0)),
        out_shape=jax.ShapeDtypeStruct((HEADS, tokens, HEAD_DIM), BF16),
        scratch_shapes=[pltpu.VMEM((n_chunks, CHUNK, HEAD_DIM), BF16),
                        pltpu.VMEM((n_chunks, CHUNK, HEAD_DIM), BF16),
                        pltpu.VMEM((seq, CHUNK), BF16)],
        compiler_params=pltpu.CompilerParams(
            dimension_semantics=("arbitrary",),
            vmem_limit_bytes=VMEM_LIMIT),
        name="retention",
    )(logits, heads, heads, heads, heads, gn_gain)


def _merge_kernel(rows_ref, uprev_ref, unext_ref, ret_ref, x_ref, convw_ref, wb_ref, wo_ref,
                  fgain_ref, out_ref, *, seq_tiles):
    tm = x_ref.shape[0]
    i = pl.program_id(0)
    pos = i % seq_tiles
    u = rows_ref[1].astype(F32)
    prev_row = jnp.where(pos == 0, 0.0, uprev_ref[0, HALO_ROWS - 1:HALO_ROWS, :].astype(F32))
    next_row = jnp.where(pos == seq_tiles - 1, 0.0, unext_ref[0, 0:1, :].astype(F32))
    ridx = lax.broadcasted_iota(jnp.int32, (tm, WIDTH), 0)
    u_prev = jnp.where(ridx == 0, prev_row, pltpu.roll(u, 1, axis=0))
    u_next = jnp.where(ridx == tm - 1, next_row, pltpu.roll(u, tm - 1, axis=0))
    w = convw_ref[...]
    conv = w[0:1, :] * u_prev + w[1:2, :] * u + w[2:3, :] * u_next
    branch_conv = (rows_ref[0].astype(F32) * conv * rows_ref[2].astype(F32)).astype(BF16)

    for r0 in range(0, tm, MERGE_SUB_ROWS):
        rs = slice(r0, r0 + MERGE_SUB_ROWS)
        branch_ret = jnp.concatenate([ret_ref[hd, rs, :] for hd in range(HEADS)], axis=1)
        halves = []
        for half in range(D_MODEL // WIDTH):
            cols = slice(half * WIDTH, (half + 1) * WIDTH)
            up_ret = jnp.dot(branch_ret, wb_ref[0, :, cols], preferred_element_type=F32)
            up_conv = jnp.dot(branch_conv[rs, :], wb_ref[1, :, cols],
                              preferred_element_type=F32)
            gate_ret = rows_ref[3 + half, rs, :].astype(F32)
            gate_conv = rows_ref[3 + D_MODEL // WIDTH + half, rs, :].astype(F32)
            halves.append((gate_ret * up_ret + gate_conv * up_conv).astype(BF16))
        merged = jnp.concatenate(halves, axis=1)
        y = x_ref[rs, :] + jnp.dot(merged, wo_ref[...], preferred_element_type=F32)
        ms = jnp.mean(y * y, axis=-1, keepdims=True)
        out_ref[rs, :] = y * lax.rsqrt(ms + EPS) * fgain_ref[...]


def _merge(rows, ret, x2, conv_w, wb_bf, wo_bf, fgain, seq):
    tokens = x2.shape[0]
    tm = MERGE_ROWS
    seq_tiles = seq // tm
    halo_per_tile = tm // HALO_ROWS
    n_halo = tokens // HALO_ROWS
    resident = pl.Buffered(1)
    return pl.pallas_call(
        functools.partial(_merge_kernel, seq_tiles=seq_tiles),
        grid=(tokens // tm,),
        in_specs=[
            pl.BlockSpec((N_ROW_GROUPS, tm, WIDTH), lambda i: (0, i, 0)),
            pl.BlockSpec((1, HALO_ROWS, WIDTH),
                         lambda i: (1, jnp.maximum(i * halo_per_tile - 1, 0), 0)),
            pl.BlockSpec((1, HALO_ROWS, WIDTH),
                         lambda i: (1, jnp.minimum((i + 1) * halo_per_tile, n_halo - 1), 0)),
            pl.BlockSpec((HEADS, tm, HEAD_DIM), lambda i: (0, i, 0)),
            pl.BlockSpec((tm, D_MODEL), lambda i: (i, 0)),
            pl.BlockSpec((CONV_K, WIDTH), lambda i: (0, 0)),
            pl.BlockSpec((2, WIDTH, D_MODEL), lambda i: (0, 0, 0), pipeline_mode=resident),
            pl.BlockSpec((D_MODEL, D_MODEL), lambda i: (0, 0), pipeline_mode=resident),
            pl.BlockSpec((1, D_MODEL), lambda i: (0, 0)),
        ],
        out_specs=pl.BlockSpec((tm, D_MODEL), lambda i: (i, 0)),
        out_shape=jax.ShapeDtypeStruct((tokens, D_MODEL), F32),
        compiler_params=pltpu.CompilerParams(
            dimension_semantics=("arbitrary",),
            vmem_limit_bytes=VMEM_LIMIT),
        name="merge",
    )(rows, rows, rows, ret, x2, conv_w, wb_bf, wo_bf, fgain)


def _rotary_tables(seq):
    pos = jnp.arange(seq, dtype=F32)
    inv_freq = ROPE_BASE ** (-jnp.arange(0, HEAD_DIM, 2, dtype=F32) / HEAD_DIM)
    ang = pos[:, None] * inv_freq[None, :]
    cos = jnp.cos(ang)
    sin = jnp.sin(ang)
    return jnp.concatenate([cos, cos], axis=1), jnp.concatenate([-sin, sin], axis=1)


def kernel(x, norm_gain, w_in, decay_logit_fwd, decay_logit_bwd, ret_gn_gain, conv_w, w_branch,
           w_out, final_gain):
    batch, seq, d_model = x.shape
    assert d_model == D_MODEL and norm_gain.shape[0] == 1
    assert w_in.shape == (1, D_MODEL, N_COL_TILES * WIDTH)
    assert seq % PROJ_ROWS == 0 and seq % MERGE_ROWS == 0 and seq % CHUNK == 0
    tokens = batch * seq
    x2 = x.reshape(tokens, D_MODEL)
    cos, sin = _rotary_tables(seq)
    heads, rows = _project(x2, norm_gain[0][None, :], w_in[0].astype(BF16), cos, sin, seq)

    logits = jnp.stack([decay_logit_fwd[0], decay_logit_bwd[0]], axis=1).astype(F32)
    logits = jnp.broadcast_to(logits[:, :, None, None], (HEADS, 2, 8, HEAD_DIM))
    gn_gain = ret_gn_gain[0].astype(F32).reshape(HEADS, 1, HEAD_DIM)
    ret = _retention(heads, logits, gn_gain, batch, seq)

    out = _merge(rows, ret, x2, conv_w[0], w_branch[0].astype(BF16), w_out[0].astype(BF16),
                 final_gain[None, :], seq)
    return out.reshape(batch, seq, D_MODEL)
```

```python
import functools

import jax
import jax.numpy as jnp
from jax import lax
from jax.experimental import pallas as pl
from jax.experimental.pallas import tpu as pltpu

F32 = jnp.float32
BF16 = jnp.bfloat16

D_MODEL = 2048
HEADS = 8
HEAD_DIM = 128
WIDTH = HEADS * HEAD_DIM
CHUNK = 128
CONV_K = 3
ROPE_BASE = 10000.0
EPS = 1e-6
G_Q, G_K, G_V, G_GRET, G_CB, G_CC, G_CX, G_GCONV, G_GATE0 = range(9)
N_COL_TILES = 12
N_HEAD_GROUPS = 4
N_ROW_GROUPS = 7

PROJ_ROWS = 1024
PROJ_SUB_ROWS = 256
MERGE_ROWS = 512
MERGE_SUB_ROWS = 256
STATE_UNROLL = 16
OUT_UNROLL = 16
HALO_ROWS = 16
VMEM_LIMIT = 56 * 1024 * 1024


def _proj_kernel(x_ref, gain_ref, w_ref, rowtab_ref, tiletab_ref, heads_ref, rows_ref, h_ref,
                 cc_ref):
    j = pl.program_id(1)

    def row_blocks(normalize=False):
        for r0 in range(0, h_ref.shape[0], PROJ_SUB_ROWS):
            rs = slice(r0, r0 + PROJ_SUB_ROWS)
            if normalize:
                x = x_ref[rs, :]
                ms = jnp.mean(x * x, axis=-1, keepdims=True)
                h = (x * lax.rsqrt(ms + EPS) * gain_ref[...]).astype(BF16)
                h_ref[rs, :] = h
            else:
                h = h_ref[rs, :]
            yield rs, jnp.dot(h, w_ref[...], preferred_element_type=F32)

    def store_heads(rs, r):
        for hd in range(HEADS):
            heads_ref[hd, rs, :] = r[:, hd * HEAD_DIM:(hd + 1) * HEAD_DIM].astype(BF16)

    def store_rotary(rs, r, scale):
        tile_cos, tile_sin, tile_ssin = (tiletab_ref[0, t:t + 1, :] for t in range(3))
        cos = tile_cos * rowtab_ref[0, rs, :] - tile_sin * rowtab_ref[1, rs, :]
        sin = tile_ssin * rowtab_ref[0, rs, :] + tile_cos * rowtab_ref[2, rs, :]
        for hd in range(HEADS):
            rh = r[:, hd * HEAD_DIM:(hd + 1) * HEAD_DIM]
            o = rh * cos + pltpu.roll(rh, HEAD_DIM // 2, axis=1) * sin
            if scale is not None:
                o = o * scale
            heads_ref[hd, rs, :] = o.astype(BF16)

    @pl.when(j == G_Q)
    def _():
        for rs, r in row_blocks(normalize=True):
            store_rotary(rs, r, None)

    @pl.when(j == G_K)
    def _():
        for rs, r in row_blocks():
            store_rotary(rs, r, HEAD_DIM ** -0.5)

    @pl.when(j == G_V)
    def _():
        for rs, r in row_blocks():
            store_heads(rs, r)

    @pl.when(j == G_GRET)
    def _():
        for rs, r in row_blocks():
            store_heads(rs, r * jax.nn.sigmoid(r))

    @pl.when(j == G_CB)
    def _():
        for rs, r in row_blocks():
            rows_ref[0, rs, :] = r.astype(BF16)

    @pl.when(j == G_CC)
    def _():
        for rs, r in row_blocks():
            cc_ref[rs, :] = r.astype(BF16)

    @pl.when(j == G_CX)
    def _():
        for rs, r in row_blocks():
            rows_ref[0, rs, :] = (cc_ref[rs, :].astype(F32) * r).astype(BF16)

    @pl.when(j == G_GCONV)
    def _():
        for rs, r in row_blocks():
            rows_ref[0, rs, :] = (r * jax.nn.sigmoid(r)).astype(BF16)

    @pl.when(j >= G_GATE0)
    def _():
        for rs, r in row_blocks():
            rows_ref[0, rs, :] = jax.nn.sigmoid(r).astype(BF16)


def _project(x2, gain, w_bf, row_tab, tile_tab, seq):
    tokens = x2.shape[0]
    tm = PROJ_ROWS
    seq_tiles = seq // tm
    grid = (tokens // tm, N_COL_TILES)
    return pl.pallas_call(
        _proj_kernel,
        grid=grid,
        in_specs=[
            pl.BlockSpec((tm, D_MODEL), lambda i, j: (i, 0)),
            pl.BlockSpec((1, D_MODEL), lambda i, j: (0, 0)),
            pl.BlockSpec((D_MODEL, WIDTH), lambda i, j: (0, j)),
            pl.BlockSpec((3, tm, HEAD_DIM), lambda i, j: (0, 0, 0)),
            pl.BlockSpec((1, 8, HEAD_DIM), lambda i, j: (i % seq_tiles, 0, 0)),
        ],
        out_specs=[
            pl.BlockSpec((HEADS, tm, HEAD_DIM),
                         lambda i, j: (jnp.minimum(j, N_HEAD_GROUPS - 1), i, 0)),
            pl.BlockSpec((1, tm, WIDTH),
                         lambda i, j: (jnp.where(j <= G_CC, 0, j - G_CC), i, 0)),
        ],
        out_shape=[
            jax.ShapeDtypeStruct((N_HEAD_GROUPS * HEADS, tokens, HEAD_DIM), BF16),
            jax.ShapeDtypeStruct((N_ROW_GROUPS, tokens, WIDTH), BF16),
        ],
        scratch_shapes=[
            pltpu.VMEM((tm, D_MODEL), BF16),
            pltpu.VMEM((tm, WIDTH), BF16),
        ],
        compiler_params=pltpu.CompilerParams(
            dimension_semantics=("arbitrary", "arbitrary"),
            vmem_limit_bytes=VMEM_LIMIT),
        name="proj",
    )(x2, gain, w_bf, row_tab, tile_tab)


def _log_sigmoid(z):
    return -(jnp.maximum(-z, 0.0) + jnp.log1p(jnp.exp(-jnp.abs(z))))


def _ret_kernel(logit_ref, q_ref, k_ref, v_ref, o_ref, sf_ref, sb_ref, p_ref, *, n_chunks):
    lf = _log_sigmoid(logit_ref[0, 0])[0:1, :]
    lb = _log_sigmoid(logit_ref[0, 1])[0:1, :]
    row = lax.broadcasted_iota(jnp.int32, (CHUNK, CHUNK), 0).astype(F32)
    col = lax.broadcasted_iota(jnp.int32, (CHUNK, CHUNK), 1).astype(F32)
    dist = row - col
    decay = jnp.exp(jnp.where(dist >= 0, dist * lf, -dist * lb))
    wq_f = jnp.exp(row * lf).astype(BF16)
    wq_b = jnp.exp((CHUNK - 1 - row) * lb).astype(BF16)
    wk_f = jnp.exp((CHUNK - row) * lf).astype(BF16)
    wk_b = jnp.exp((row + 1) * lb).astype(BF16)
    chunk_f = jnp.exp(CHUNK * lf)
    chunk_b = jnp.exp(CHUNK * lb)

    def rows_of(c):
        return pl.ds(pl.multiple_of(c * CHUNK, CHUNK), CHUNK)

    def kv_summary(k_weighted, v):
        return lax.dot_general(k_weighted, v, (((0,), (0,)), ((), ())),
                               preferred_element_type=F32)

    def state_body(i, carry):
        state_f, state_b = carry
        cf = i
        cb = n_chunks - 1 - i
        sf_ref[cf] = state_f.astype(BF16)
        sb_ref[cb] = state_b.astype(BF16)
        rows_f = rows_of(cf)
        rows_b = rows_of(cb)
        k_f = k_ref[0, rows_f, :]
        scores = lax.dot_general(q_ref[0, rows_f, :], k_f, (((1,), (1,)), ((), ())),
                                 preferred_element_type=F32)
        p_ref[rows_f, :] = (scores * decay).astype(BF16)
        return (chunk_f * state_f + kv_summary(k_f * wk_f, v_ref[0, rows_f, :]),
                chunk_b * state_b + kv_summary(k_ref[0, rows_b, :] * wk_b, v_ref[0, rows_b, :]))

    zero = jnp.zeros((CHUNK, HEAD_DIM), F32)
    lax.fori_loop(0, n_chunks, state_body, (zero, zero), unroll=STATE_UNROLL)

    def out_body(c, carry):
        rows = rows_of(c)
        q = q_ref[0, rows, :]
        lhs = jnp.concatenate([p_ref[rows, :], q * wq_f, q * wq_b], axis=1)
        rhs = jnp.concatenate([v_ref[0, rows, :], sf_ref[c], sb_ref[c]], axis=0)
        o = jnp.dot(lhs, rhs, preferred_element_type=F32)
        ms = jnp.mean(o * o, axis=-1, keepdims=True)
        o_ref[0, rows, :] = (o * lax.rsqrt(ms + EPS)).astype(BF16)
        return carry

    lax.fori_loop(0, n_chunks, out_body, 0, unroll=OUT_UNROLL)


def _retention(heads, logits, batch, seq):
    tokens = batch * seq
    n_chunks = seq // CHUNK

    def head_spec(group):
        return pl.BlockSpec((1, seq, HEAD_DIM),
                            lambda p: (group * HEADS + p % HEADS, p // HEADS, 0))

    return pl.pallas_call(
        functools.partial(_ret_kernel, n_chunks=n_chunks),
        grid=(batch * HEADS,),
        in_specs=[
            pl.BlockSpec((1, 2, 8, HEAD_DIM), lambda p: (p % HEADS, 0, 0, 0)),
            head_spec(G_Q), head_spec(G_K), head_spec(G_V),
        ],
        out_specs=pl.BlockSpec((1, seq, HEAD_DIM), lambda p: (p % HEADS, p // HEADS, 0)),
        out_shape=jax.ShapeDtypeStruct((HEADS, tokens, HEAD_DIM), BF16),
        scratch_shapes=[pltpu.VMEM((n_chunks, CHUNK, HEAD_DIM), BF16),
                        pltpu.VMEM((n_chunks, CHUNK, HEAD_DIM), BF16),
                        pltpu.VMEM((seq, CHUNK), BF16)],
        compiler_params=pltpu.CompilerParams(
            dimension_semantics=("arbitrary",),
            vmem_limit_bytes=VMEM_LIMIT),
        name="retention",
    )(logits, heads, heads, heads)


def _merge_kernel(rows_ref, uprev_ref, unext_ref, ret_ref, gret_ref, gngain_ref, x_ref, convw_ref,
                  wb_ref, wo_ref, fgain_ref, out_ref, *, seq_tiles):
    tm = x_ref.shape[0]
    i = pl.program_id(0)
    pos = i % seq_tiles
    u = rows_ref[1].astype(F32)
    prev_row = jnp.where(pos == 0, 0.0, uprev_ref[0, HALO_ROWS - 1:HALO_ROWS, :].astype(F32))
    next_row = jnp.where(pos == seq_tiles - 1, 0.0, unext_ref[0, 0:1, :].astype(F32))
    ridx = lax.broadcasted_iota(jnp.int32, (tm, WIDTH), 0)
    u_prev = jnp.where(ridx == 0, prev_row, pltpu.roll(u, 1, axis=0))
    u_next = jnp.where(ridx == tm - 1, next_row, pltpu.roll(u, tm - 1, axis=0))
    w = convw_ref[...]
    conv = w[0:1, :] * u_prev + w[1:2, :] * u + w[2:3, :] * u_next
    branch_conv = (rows_ref[0].astype(F32) * conv * rows_ref[2].astype(F32)).astype(BF16)

    for r0 in range(0, tm, MERGE_SUB_ROWS):
        rs = slice(r0, r0 + MERGE_SUB_ROWS)
        branch_ret = jnp.concatenate(
            [(ret_ref[hd, rs, :].astype(F32) * gngain_ref[hd] * gret_ref[hd, rs, :].astype(F32)
              ).astype(BF16) for hd in range(HEADS)], axis=1)
        halves = []
        for half in range(D_MODEL // WIDTH):
            cols = slice(half * WIDTH, (half + 1) * WIDTH)
            up_ret = jnp.dot(branch_ret, wb_ref[0, :, cols], preferred_element_type=F32)
            up_conv = jnp.dot(branch_conv[rs, :], wb_ref[1, :, cols],
                              preferred_element_type=F32)
            gate_ret = rows_ref[3 + half, rs, :].astype(F32)
            gate_conv = rows_ref[3 + D_MODEL // WIDTH + half, rs, :].astype(F32)
            halves.append((gate_ret * up_ret + gate_conv * up_conv).astype(BF16))
        merged = jnp.concatenate(halves, axis=1)
        y = x_ref[rs, :] + jnp.dot(merged, wo_ref[...], preferred_element_type=F32)
        ms = jnp.mean(y * y, axis=-1, keepdims=True)
        out_ref[rs, :] = y * lax.rsqrt(ms + EPS) * fgain_ref[...]


def _merge(rows, ret, heads, gn_gain, x2, conv_w, wb_bf, wo_bf, fgain, seq):
    tokens = x2.shape[0]
    tm = MERGE_ROWS
    seq_tiles = seq // tm
    halo_per_tile = tm // HALO_ROWS
    n_halo = tokens // HALO_ROWS
    resident = pl.Buffered(1)
    return pl.pallas_call(
        functools.partial(_merge_kernel, seq_tiles=seq_tiles),
        grid=(tokens // tm,),
        in_specs=[
            pl.BlockSpec((N_ROW_GROUPS, tm, WIDTH), lambda i: (0, i, 0)),
            pl.BlockSpec((1, HALO_ROWS, WIDTH),
                         lambda i: (1, jnp.maximum(i * halo_per_tile - 1, 0), 0)),
            pl.BlockSpec((1, HALO_ROWS, WIDTH),
                         lambda i: (1, jnp.minimum((i + 1) * halo_per_tile, n_halo - 1), 0)),
            pl.BlockSpec((HEADS, tm, HEAD_DIM), lambda i: (0, i, 0)),
            pl.BlockSpec((HEADS, tm, HEAD_DIM), lambda i: (G_GRET, i, 0)),
            pl.BlockSpec((HEADS, 1, HEAD_DIM), lambda i: (0, 0, 0)),
            pl.BlockSpec((tm, D_MODEL), lambda i: (i, 0)),
            pl.BlockSpec((CONV_K, WIDTH), lambda i: (0, 0)),
            pl.BlockSpec((2, WIDTH, D_MODEL), lambda i: (0, 0, 0), pipeline_mode=resident),
            pl.BlockSpec((D_MODEL, D_MODEL), lambda i: (0, 0), pipeline_mode=resident),
            pl.BlockSpec((1, D_MODEL), lambda i: (0, 0)),
        ],
        out_specs=pl.BlockSpec((tm, D_MODEL), lambda i: (i, 0)),
        out_shape=jax.ShapeDtypeStruct((tokens, D_MODEL), F32),
        compiler_params=pltpu.CompilerParams(
            dimension_semantics=("arbitrary",),
            vmem_limit_bytes=VMEM_LIMIT),
        name="merge",
    )(rows, rows, rows, ret, heads, gn_gain, x2, conv_w, wb_bf, wo_bf, fgain)


def _rotary_tables(seq, tile_rows):
    inv_freq = ROPE_BASE ** (-jnp.arange(0, HEAD_DIM, 2, dtype=F32) / HEAD_DIM)
    inv_freq = jnp.concatenate([inv_freq, inv_freq])
    sign = jnp.where(jnp.arange(HEAD_DIM) < HEAD_DIM // 2, -1.0, 1.0).astype(F32)

    def table(pos):
        ang = pos.astype(F32)[:, None] * inv_freq[None, :]
        return jnp.stack([jnp.cos(ang), jnp.sin(ang), sign * jnp.sin(ang)])

    row_tab = table(jnp.arange(tile_rows))
    tile_tab = jnp.transpose(table(jnp.arange(seq // tile_rows) * tile_rows), (1, 0, 2))
    return row_tab, jnp.pad(tile_tab, ((0, 0), (0, 8 - 3), (0, 0)))


def kernel(x, norm_gain, w_in, decay_logit_fwd, decay_logit_bwd, ret_gn_gain, conv_w, w_branch,
           w_out, final_gain):
    batch, seq, d_model = x.shape
    assert d_model == D_MODEL and norm_gain.shape[0] == 1
    assert w_in.shape == (1, D_MODEL, N_COL_TILES * WIDTH)
    assert seq % PROJ_ROWS == 0 and seq % MERGE_ROWS == 0 and seq % CHUNK == 0
    tokens = batch * seq
    x2 = x.reshape(tokens, D_MODEL)
    row_tab, tile_tab = _rotary_tables(seq, PROJ_ROWS)
    heads, rows = _project(x2, norm_gain[0][None, :], w_in[0].astype(BF16), row_tab, tile_tab, seq)

    logits = jnp.stack([decay_logit_fwd[0], decay_logit_bwd[0]], axis=1).astype(F32)
    logits = jnp.broadcast_to(logits[:, :, None, None], (HEADS, 2, 8, HEAD_DIM))
    gn_gain = ret_gn_gain[0].astype(F32).reshape(HEADS, 1, HEAD_DIM)
    ret = _retention(heads, logits, batch, seq)

    out = _merge(rows, ret, heads, gn_gain, x2, conv_w[0], w_branch[0].astype(BF16),
                 w_out[0].astype(BF16), final_gain[None, :], seq)
    return out.reshape(batch, seq, D_MODEL)
```

```python
import functools

import jax
import jax.numpy as jnp
from jax import lax
from jax.experimental import pallas as pl
from jax.experimental.pallas import tpu as pltpu

F32 = jnp.float32
BF16 = jnp.bfloat16

D_MODEL = 2048
HEADS = 8
HEAD_DIM = 128
WIDTH = HEADS * HEAD_DIM
CHUNK = 128
CONV_K = 3
ROPE_BASE = 10000.0
EPS = 1e-6
G_Q, G_K, G_V, G_GRET, G_CB, G_CC, G_CX, G_GCONV, G_GATE0 = range(9)
N_COL_TILES = 12
N_HEAD_GROUPS = 4
R_U, R_CBG, R_GATE0 = 0, 1, 2
N_ROW_GROUPS = 6

PROJ_ROWS = 1024
PROJ_SUB_ROWS = 256
MERGE_ROWS = 512
MERGE_SUB_ROWS = 256
STATE_UNROLL = 32
OUT_UNROLL = 32
HALO_ROWS = 16
VMEM_LIMIT = 56 * 1024 * 1024


def _proj_kernel(x_ref, gain_ref, w_ref, rowtab_ref, tiletab_ref, heads_ref, rows_ref, h_ref,
                 cb_ref, cc_ref):
    j = pl.program_id(1)

    def row_blocks(normalize=False):
        for r0 in range(0, h_ref.shape[0], PROJ_SUB_ROWS):
            rs = slice(r0, r0 + PROJ_SUB_ROWS)
            if normalize:
                x = x_ref[rs, :]
                ms = jnp.mean(x * x, axis=-1, keepdims=True)
                h = (x * lax.rsqrt(ms + EPS) * gain_ref[...]).astype(BF16)
                h_ref[rs, :] = h
            else:
                h = h_ref[rs, :]
            yield rs, jnp.dot(h, w_ref[...], preferred_element_type=F32)

    def store_heads(rs, r):
        for hd in range(HEADS):
            heads_ref[hd, rs, :] = r[:, hd * HEAD_DIM:(hd + 1) * HEAD_DIM].astype(BF16)

    def store_rotary(rs, r, scale):
        tile_cos, tile_sin, tile_ssin = (tiletab_ref[0, t:t + 1, :] for t in range(3))
        cos = tile_cos * rowtab_ref[0, rs, :] - tile_sin * rowtab_ref[1, rs, :]
        sin = tile_ssin * rowtab_ref[0, rs, :] + tile_cos * rowtab_ref[2, rs, :]
        for hd in range(HEADS):
            rh = r[:, hd * HEAD_DIM:(hd + 1) * HEAD_DIM]
            o = rh * cos + pltpu.roll(rh, HEAD_DIM // 2, axis=1) * sin
            if scale is not None:
                o = o * scale
            heads_ref[hd, rs, :] = o.astype(BF16)

    @pl.when(j == G_Q)
    def _():
        for rs, r in row_blocks(normalize=True):
            store_rotary(rs, r, None)

    @pl.when(j == G_K)
    def _():
        for rs, r in row_blocks():
            store_rotary(rs, r, HEAD_DIM ** -0.5)

    @pl.when(j == G_V)
    def _():
        for rs, r in row_blocks():
            store_heads(rs, r)

    @pl.when(j == G_GRET)
    def _():
        for rs, r in row_blocks():
            store_heads(rs, r * jax.nn.sigmoid(r))

    @pl.when(j == G_CB)
    def _():
        for rs, r in row_blocks():
            cb_ref[rs, :] = r.astype(BF16)

    @pl.when(j == G_CC)
    def _():
        for rs, r in row_blocks():
            cc_ref[rs, :] = r.astype(BF16)

    @pl.when(j == G_CX)
    def _():
        for rs, r in row_blocks():
            rows_ref[0, rs, :] = (cc_ref[rs, :].astype(F32) * r).astype(BF16)

    @pl.when(j == G_GCONV)
    def _():
        for rs, r in row_blocks():
            rows_ref[0, rs, :] = (cb_ref[rs, :].astype(F32) * (r * jax.nn.sigmoid(r))
                                  ).astype(BF16)

    @pl.when(j >= G_GATE0)
    def _():
        for rs, r in row_blocks():
            rows_ref[0, rs, :] = jax.nn.sigmoid(r).astype(BF16)


def _project(x2, gain, w_bf, row_tab, tile_tab, seq):
    tokens = x2.shape[0]
    tm = PROJ_ROWS
    seq_tiles = seq // tm
    grid = (tokens // tm, N_COL_TILES)
    return pl.pallas_call(
        _proj_kernel,
        grid=grid,
        in_specs=[
            pl.BlockSpec((tm, D_MODEL), lambda i, j: (i, 0)),
            pl.BlockSpec((1, D_MODEL), lambda i, j: (0, 0)),
            pl.BlockSpec((D_MODEL, WIDTH), lambda i, j: (0, j)),
            pl.BlockSpec((3, tm, HEAD_DIM), lambda i, j: (0, 0, 0)),
            pl.BlockSpec((1, 8, HEAD_DIM), lambda i, j: (i % seq_tiles, 0, 0)),
        ],
        out_specs=[
            pl.BlockSpec((HEADS, tm, HEAD_DIM),
                         lambda i, j: (jnp.minimum(j, N_HEAD_GROUPS - 1), i, 0)),
            pl.BlockSpec((1, tm, WIDTH),
                         lambda i, j: (jnp.maximum(j - G_CX, 0), i, 0)),
        ],
        out_shape=[
            jax.ShapeDtypeStruct((N_HEAD_GROUPS * HEADS, tokens, HEAD_DIM), BF16),
            jax.ShapeDtypeStruct((N_ROW_GROUPS, tokens, WIDTH), BF16),
        ],
        scratch_shapes=[
            pltpu.VMEM((tm, D_MODEL), BF16),
            pltpu.VMEM((tm, WIDTH), BF16),
            pltpu.VMEM((tm, WIDTH), BF16),
        ],
        compiler_params=pltpu.CompilerParams(
            dimension_semantics=("arbitrary", "arbitrary"),
            vmem_limit_bytes=VMEM_LIMIT),
        name="proj",
    )(x2, gain, w_bf, row_tab, tile_tab)


def _log_sigmoid(z):
    return -(jnp.maximum(-z, 0.0) + jnp.log1p(jnp.exp(-jnp.abs(z))))


def _ret_kernel(logit_ref, q_ref, k_ref, v_ref, g_ref, gain_ref, o_ref, sf_ref, sb_ref, p_ref, *,
                n_chunks):
    lf = _log_sigmoid(logit_ref[0, 0])[0:1, :]
    lb = _log_sigmoid(logit_ref[0, 1])[0:1, :]
    row = lax.broadcasted_iota(jnp.int32, (CHUNK, CHUNK), 0).astype(F32)
    col = lax.broadcasted_iota(jnp.int32, (CHUNK, CHUNK), 1).astype(F32)
    dist = row - col
    decay = jnp.exp(jnp.where(dist >= 0, dist * lf, -dist * lb))
    wq_f = jnp.exp(row * lf).astype(BF16)
    wq_b = jnp.exp((CHUNK - 1 - row) * lb).astype(BF16)
    wk_f = jnp.exp((CHUNK - row) * lf).astype(BF16)
    wk_b = jnp.exp((row + 1) * lb).astype(BF16)
    chunk_f = jnp.exp(CHUNK * lf)
    chunk_b = jnp.exp(CHUNK * lb)
    gain = gain_ref[0]

    def rows_of(c):
        return pl.ds(pl.multiple_of(c * CHUNK, CHUNK), CHUNK)

    def kv_summary(k_weighted, v):
        return lax.dot_general(k_weighted, v, (((0,), (0,)), ((), ())),
                               preferred_element_type=F32)

    def state_body(i, carry):
        state_f, state_b = carry
        cf = i
        cb = n_chunks - 1 - i
        sf_ref[cf] = state_f.astype(BF16)
        sb_ref[cb] = state_b.astype(BF16)
        rows_f = rows_of(cf)
        rows_b = rows_of(cb)
        k_f = k_ref[0, rows_f, :]
        scores = lax.dot_general(q_ref[0, rows_f, :], k_f, (((1,), (1,)), ((), ())),
                                 preferred_element_type=F32)
        p_ref[rows_f, :] = (scores * decay).astype(BF16)
        return (chunk_f * state_f + kv_summary(k_f * wk_f, v_ref[0, rows_f, :]),
                chunk_b * state_b + kv_summary(k_ref[0, rows_b, :] * wk_b, v_ref[0, rows_b, :]))

    zero = jnp.zeros((CHUNK, HEAD_DIM), F32)
    lax.fori_loop(0, n_chunks, state_body, (zero, zero), unroll=STATE_UNROLL)

    def out_body(c, carry):
        rows = rows_of(c)
        q = q_ref[0, rows, :]
        lhs = jnp.concatenate([p_ref[rows, :], q * wq_f, q * wq_b], axis=1)
        rhs = jnp.concatenate([v_ref[0, rows, :], sf_ref[c], sb_ref[c]], axis=0)
        o = jnp.dot(lhs, rhs, preferred_element_type=F32)
        ms = jnp.mean(o * o, axis=-1, keepdims=True)
        o = o * lax.rsqrt(ms + EPS) * gain * g_ref[0, rows, :].astype(F32)
        o_ref[0, rows, :] = o.astype(BF16)
        return carry

    lax.fori_loop(0, n_chunks, out_body, 0, unroll=OUT_UNROLL)


def _retention(heads, logits, gn_gain, batch, seq):
    tokens = batch * seq
    n_chunks = seq // CHUNK

    def head_spec(group):
        return pl.BlockSpec((1, seq, HEAD_DIM),
                            lambda p: (group * HEADS + p % HEADS, p // HEADS, 0))

    return pl.pallas_call(
        functools.partial(_ret_kernel, n_chunks=n_chunks),
        grid=(batch * HEADS,),
        in_specs=[
            pl.BlockSpec((1, 2, 8, HEAD_DIM), lambda p: (p % HEADS, 0, 0, 0)),
            head_spec(G_Q), head_spec(G_K), head_spec(G_V), head_spec(G_GRET),
            pl.BlockSpec((1, 1, HEAD_DIM), lambda p: (p % HEADS, 0, 0)),
        ],
        out_specs=pl.BlockSpec((1, seq, HEAD_DIM), lambda p: (p % HEADS, p // HEADS, 0)),
        out_shape=jax.ShapeDtypeStruct((HEADS, tokens, HEAD_DIM), BF16),
        scratch_shapes=[pltpu.VMEM((n_chunks, CHUNK, HEAD_DIM), BF16),
                        pltpu.VMEM((n_chunks, CHUNK, HEAD_DIM), BF16),
                        pltpu.VMEM((seq, CHUNK), BF16)],
        compiler_params=pltpu.CompilerParams(
            dimension_semantics=("arbitrary",),
            vmem_limit_bytes=VMEM_LIMIT),
        name="retention",
    )(logits, heads, heads, heads, heads, gn_gain)


def _merge_kernel(rows_ref, uprev_ref, unext_ref, ret_ref, x_ref, convw_ref, wb_ref, wo_ref,
                  fgain_ref, out_ref, *, seq_tiles):
    tm = x_ref.shape[0]
    i = pl.program_id(0)
    pos = i % seq_tiles
    u = rows_ref[R_U].astype(F32)
    prev_row = jnp.where(pos == 0, 0.0, uprev_ref[0, HALO_ROWS - 1:HALO_ROWS, :].astype(F32))
    next_row = jnp.where(pos == seq_tiles - 1, 0.0, unext_ref[0, 0:1, :].astype(F32))
    ridx = lax.broadcasted_iota(jnp.int32, (tm, WIDTH), 0)
    u_prev = jnp.where(ridx == 0, prev_row, pltpu.roll(u, 1, axis=0))
    u_next = jnp.where(ridx == tm - 1, next_row, pltpu.roll(u, tm - 1, axis=0))
    w = convw_ref[...]
    conv = w[0:1, :] * u_prev + w[1:2, :] * u + w[2:3, :] * u_next
    branch_conv = (rows_ref[R_CBG].astype(F32) * conv).astype(BF16)

    for r0 in range(0, tm, MERGE_SUB_ROWS):
        rs = slice(r0, r0 + MERGE_SUB_ROWS)
        branch_ret = jnp.concatenate([ret_ref[hd, rs, :] for hd in range(HEADS)], axis=1)
        halves = []
        for half in range(D_MODEL // WIDTH):
            cols = slice(half * WIDTH, (half + 1) * WIDTH)
            up_ret = jnp.dot(branch_ret, wb_ref[0, :, cols], preferred_element_type=F32)
            up_conv = jnp.dot(branch_conv[rs, :], wb_ref[1, :, cols],
                              preferred_element_type=F32)
            gate_ret = rows_ref[R_GATE0 + half, rs, :].astype(F32)
            gate_conv = rows_ref[R_GATE0 + D_MODEL // WIDTH + half, rs, :].astype(F32)
            halves.append((gate_ret * up_ret + gate_conv * up_conv).astype(BF16))
        merged = jnp.concatenate(halves, axis=1)
        y = x_ref[rs, :] + jnp.dot(merged, wo_ref[...], preferred_element_type=F32)
        ms = jnp.mean(y * y, axis=-1, keepdims=True)
        out_ref[rs, :] = y * lax.rsqrt(ms + EPS) * fgain_ref[...]


def _merge(rows, ret, x2, conv_w, wb_bf, wo_bf, fgain, seq):
    tokens = x2.shape[0]
    tm = MERGE_ROWS
    seq_tiles = seq // tm
    halo_per_tile = tm // HALO_ROWS
    n_halo = tokens // HALO_ROWS
    resident = pl.Buffered(1)
    return pl.pallas_call(
        functools.partial(_merge_kernel, seq_tiles=seq_tiles),
        grid=(tokens // tm,),
        in_specs=[
            pl.BlockSpec((N_ROW_GROUPS, tm, WIDTH), lambda i: (0, i, 0)),
            pl.BlockSpec((1, HALO_ROWS, WIDTH),
                         lambda i: (R_U, jnp.maximum(i * halo_per_tile - 1, 0), 0)),
            pl.BlockSpec((1, HALO_ROWS, WIDTH),
                         lambda i: (R_U, jnp.minimum((i + 1) * halo_per_tile, n_halo - 1), 0)),
            pl.BlockSpec((HEADS, tm, HEAD_DIM), lambda i: (0, i, 0)),
            pl.BlockSpec((tm, D_MODEL), lambda i: (i, 0)),
            pl.BlockSpec((CONV_K, WIDTH), lambda i: (0, 0)),
            pl.BlockSpec((2, WIDTH, D_MODEL), lambda i: (0, 0, 0), pipeline_mode=resident),
            pl.BlockSpec((D_MODEL, D_MODEL), lambda i: (0, 0), pipeline_mode=resident),
            pl.BlockSpec((1, D_MODEL), lambda i: (0, 0)),
        ],
        out_specs=pl.BlockSpec((tm, D_MODEL), lambda i: (i, 0)),
        out_shape=jax.ShapeDtypeStruct((tokens, D_MODEL), F32),
        compiler_params=pltpu.CompilerParams(
            dimension_semantics=("arbitrary",),
            vmem_limit_bytes=VMEM_LIMIT),
        name="merge",
    )(rows, rows, rows, ret, x2, conv_w, wb_bf, wo_bf, fgain)


def _rotary_tables(seq, tile_rows):
    inv_freq = ROPE_BASE ** (-jnp.arange(0, HEAD_DIM, 2, dtype=F32) / HEAD_DIM)
    inv_freq = jnp.concatenate([inv_freq, inv_freq])
    sign = jnp.where(jnp.arange(HEAD_DIM) < HEAD_DIM // 2, -1.0, 1.0).astype(F32)

    def table(pos):
        ang = pos.astype(F32)[:, None] * inv_freq[None, :]
        return jnp.stack([jnp.cos(ang), jnp.sin(ang), sign * jnp.sin(ang)])

    row_tab = table(jnp.arange(tile_rows))
    tile_tab = jnp.transpose(table(jnp.arange(seq // tile_rows) * tile_rows), (1, 0, 2))
    return row_tab, jnp.pad(tile_tab, ((0, 0), (0, 8 - 3), (0, 0)))


def kernel(x, norm_gain, w_in, decay_logit_fwd, decay_logit_bwd, ret_gn_gain, conv_w, w_branch,
           w_out, final_gain):
    batch, seq, d_model = x.shape
    assert d_model == D_MODEL and norm_gain.shape[0] == 1
    assert w_in.shape == (1, D_MODEL, N_COL_TILES * WIDTH)
    assert seq % PROJ_ROWS == 0 and seq % MERGE_ROWS == 0 and seq % CHUNK == 0
    tokens = batch * seq
    x2 = x.reshape(tokens, D_MODEL)
    row_tab, tile_tab = _rotary_tables(seq, PROJ_ROWS)
    heads, rows = _project(x2, norm_gain[0][None, :], w_in[0].astype(BF16), row_tab, tile_tab, seq)

    logits = jnp.stack([decay_logit_fwd[0], decay_logit_bwd[0]], axis=1).astype(F32)
    logits = jnp.broadcast_to(logits[:, :, None, None], (HEADS, 2, 8, HEAD_DIM))
    gn_gain = ret_gn_gain[0].astype(F32).reshape(HEADS, 1, HEAD_DIM)
    ret = _retention(heads, logits, gn_gain, batch, seq)

    out = _merge(rows, ret, x2, conv_w[0], w_branch[0].astype(BF16), w_out[0].astype(BF16),
                 final_gain[None, :], seq)
    return out.reshape(batch, seq, D_MODEL)
```

```python
import functools

import jax
import jax.numpy as jnp
from jax import lax
from jax.experimental import pallas as pl
from jax.experimental.pallas import tpu as pltpu

F32 = jnp.float32
BF16 = jnp.bfloat16

D_MODEL = 2048
HEADS = 8
HEAD_DIM = 128
WIDTH = HEADS * HEAD_DIM
CHUNK = 128
CONV_K = 3
ROPE_BASE = 10000.0
EPS = 1e-6
G_Q, G_K, G_V, G_GRET, G_CB, G_CC, G_CX, G_GCONV, G_GATE0 = range(9)
N_COL_TILES = 12
N_HEAD_GROUPS = 4
R_U, R_CBG, R_GATE0 = 0, 1, 2
N_ROW_GROUPS = 6

PROJ_ROWS = 1024
PROJ_SUB_ROWS = 256
PROJ_PLAIN_SUB_ROWS = 512
MERGE_ROWS = 512
MERGE_SUB_ROWS = 256
STATE_UNROLL = 64
OUT_UNROLL = 64
HALO_ROWS = 16
VMEM_LIMIT = 58 * 1024 * 1024


def _proj_kernel(x_ref, gain_ref, w_ref, rowtab_ref, tiletab_ref, heads_ref, rows_ref, h_ref,
                 cb_ref, cc_ref):
    j = pl.program_id(1)

    def row_blocks(normalize=False, sub_rows=PROJ_SUB_ROWS):
        for r0 in range(0, h_ref.shape[0], sub_rows):
            rs = slice(r0, r0 + sub_rows)
            if normalize:
                x = x_ref[rs, :]
                ms = jnp.mean(x * x, axis=-1, keepdims=True)
                h = (x * lax.rsqrt(ms + EPS) * gain_ref[...]).astype(BF16)
                h_ref[rs, :] = h
            else:
                h = h_ref[rs, :]
            yield rs, jnp.dot(h, w_ref[...], preferred_element_type=F32)

    def store_heads(rs, r):
        for hd in range(HEADS):
            heads_ref[hd, rs, :] = r[:, hd * HEAD_DIM:(hd + 1) * HEAD_DIM].astype(BF16)

    def store_rotary(rs, r, scale):
        tile_cos, tile_sin, tile_ssin = (tiletab_ref[0, t:t + 1, :] for t in range(3))
        cos = tile_cos * rowtab_ref[0, rs, :] - tile_sin * rowtab_ref[1, rs, :]
        sin = tile_ssin * rowtab_ref[0, rs, :] + tile_cos * rowtab_ref[2, rs, :]
        for hd in range(HEADS):
            rh = r[:, hd * HEAD_DIM:(hd + 1) * HEAD_DIM]
            o = rh * cos + pltpu.roll(rh, HEAD_DIM // 2, axis=1) * sin
            if scale is not None:
                o = o * scale
            heads_ref[hd, rs, :] = o.astype(BF16)

    @pl.when(j == G_Q)
    def _():
        for rs, r in row_blocks(normalize=True):
            store_rotary(rs, r, None)

    @pl.when(j == G_K)
    def _():
        for rs, r in row_blocks():
            store_rotary(rs, r, HEAD_DIM ** -0.5)

    @pl.when(j == G_V)
    def _():
        for rs, r in row_blocks(sub_rows=PROJ_PLAIN_SUB_ROWS):
            store_heads(rs, r)

    @pl.when(j == G_GRET)
    def _():
        for rs, r in row_blocks():
            store_heads(rs, r * jax.nn.sigmoid(r))

    @pl.when(j == G_CB)
    def _():
        for rs, r in row_blocks(sub_rows=PROJ_PLAIN_SUB_ROWS):
            cb_ref[rs, :] = r.astype(BF16)

    @pl.when(j == G_CC)
    def _():
        for rs, r in row_blocks(sub_rows=PROJ_PLAIN_SUB_ROWS):
            cc_ref[rs, :] = r.astype(BF16)

    @pl.when(j == G_CX)
    def _():
        for rs, r in row_blocks(sub_rows=PROJ_PLAIN_SUB_ROWS):
            rows_ref[0, rs, :] = (cc_ref[rs, :].astype(F32) * r).astype(BF16)

    @pl.when(j == G_GCONV)
    def _():
        for rs, r in row_blocks():
            rows_ref[0, rs, :] = (cb_ref[rs, :].astype(F32) * (r * jax.nn.sigmoid(r))
                                  ).astype(BF16)

    @pl.when(j >= G_GATE0)
    def _():
        for rs, r in row_blocks():
            rows_ref[0, rs, :] = jax.nn.sigmoid(r).astype(BF16)


def _project(x2, gain, w_bf, row_tab, tile_tab, seq):
    tokens = x2.shape[0]
    tm = PROJ_ROWS
    seq_tiles = seq // tm
    grid = (tokens // tm, N_COL_TILES)
    return pl.pallas_call(
        _proj_kernel,
        grid=grid,
        in_specs=[
            pl.BlockSpec((tm, D_MODEL), lambda i, j: (i, 0)),
            pl.BlockSpec((1, D_MODEL), lambda i, j: (0, 0)),
            pl.BlockSpec((D_MODEL, WIDTH), lambda i, j: (0, j)),
            pl.BlockSpec((3, tm, HEAD_DIM), lambda i, j: (0, 0, 0)),
            pl.BlockSpec((1, 8, HEAD_DIM), lambda i, j: (i % seq_tiles, 0, 0)),
        ],
        out_specs=[
            pl.BlockSpec((HEADS, tm, HEAD_DIM),
                         lambda i, j: (jnp.minimum(j, N_HEAD_GROUPS - 1), i, 0)),
            pl.BlockSpec((1, tm, WIDTH),
                         lambda i, j: (jnp.maximum(j - G_CX, 0), i, 0)),
        ],
        out_shape=[
            jax.ShapeDtypeStruct((N_HEAD_GROUPS * HEADS, tokens, HEAD_DIM), BF16),
            jax.ShapeDtypeStruct((N_ROW_GROUPS, tokens, WIDTH), BF16),
        ],
        scratch_shapes=[
            pltpu.VMEM((tm, D_MODEL), BF16),
            pltpu.VMEM((tm, WIDTH), BF16),
            pltpu.VMEM((tm, WIDTH), BF16),
        ],
        compiler_params=pltpu.CompilerParams(
            dimension_semantics=("arbitrary", "arbitrary"),
            vmem_limit_bytes=VMEM_LIMIT),
        name="proj",
    )(x2, gain, w_bf, row_tab, tile_tab)


def _log_sigmoid(z):
    return -(jnp.maximum(-z, 0.0) + jnp.log1p(jnp.exp(-jnp.abs(z))))


def _ret_kernel(logit_ref, q_ref, k_ref, v_ref, g_ref, gain_ref, o_ref, sf_ref, sb_ref, p_ref, *,
                n_chunks):
    lf = _log_sigmoid(logit_ref[0, 0])[0:1, :]
    lb = _log_sigmoid(logit_ref[0, 1])[0:1, :]
    row = lax.broadcasted_iota(jnp.int32, (CHUNK, CHUNK), 0).astype(F32)
    col = lax.broadcasted_iota(jnp.int32, (CHUNK, CHUNK), 1).astype(F32)
    dist = row - col
    decay = jnp.exp(jnp.where(dist >= 0, dist * lf, -dist * lb))
    wq_f = jnp.exp(row * lf).astype(BF16)
    wq_b = jnp.exp((CHUNK - 1 - row) * lb).astype(BF16)
    wk_f = jnp.exp((CHUNK - row) * lf).astype(BF16)
    wk_b = jnp.exp((row + 1) * lb).astype(BF16)
    chunk_f = jnp.exp(CHUNK * lf)
    chunk_b = jnp.exp(CHUNK * lb)
    gain = gain_ref[0]

    def rows_of(c):
        return pl.ds(pl.multiple_of(c * CHUNK, CHUNK), CHUNK)

    def kv_summary(k_weighted, v):
        return lax.dot_general(k_weighted, v, (((0,), (0,)), ((), ())),
                               preferred_element_type=F32)

    def state_body(i, carry):
        state_f, state_b = carry
        cf = i
        cb = n_chunks - 1 - i
        sf_ref[cf] = state_f.astype(BF16)
        sb_ref[cb] = state_b.astype(BF16)
        rows_f = rows_of(cf)
        rows_b = rows_of(cb)
        k_f = k_ref[0, rows_f, :]
        scores = lax.dot_general(q_ref[0, rows_f, :], k_f, (((1,), (1,)), ((), ())),
                                 preferred_element_type=F32)
        p_ref[rows_f, :] = (scores * decay).astype(BF16)
        return (chunk_f * state_f + kv_summary(k_f * wk_f, v_ref[0, rows_f, :]),
                chunk_b * state_b + kv_summary(k_ref[0, rows_b, :] * wk_b, v_ref[0, rows_b, :]))

    zero = jnp.zeros((CHUNK, HEAD_DIM), F32)
    lax.fori_loop(0, n_chunks, state_body, (zero, zero), unroll=STATE_UNROLL)

    def out_body(c, carry):
        rows = rows_of(c)
        q = q_ref[0, rows, :]
        lhs = jnp.concatenate([p_ref[rows, :], q * wq_f, q * wq_b], axis=1)
        rhs = jnp.concatenate([v_ref[0, rows, :], sf_ref[c], sb_ref[c]], axis=0)
        o = jnp.dot(lhs, rhs, preferred_element_type=F32)
        ms = jnp.mean(o * o, axis=-1, keepdims=True)
        o = o * lax.rsqrt(ms + EPS) * gain * g_ref[0, rows, :].astype(F32)
        o_ref[0, rows, :] = o.astype(BF16)
        return carry

    lax.fori_loop(0, n_chunks, out_body, 0, unroll=OUT_UNROLL)


def _retention(heads, logits, gn_gain, batch, seq):
    tokens = batch * seq
    n_chunks = seq // CHUNK

    def head_spec(group):
        return pl.BlockSpec((1, seq, HEAD_DIM),
                            lambda p: (group * HEADS + p % HEADS, p // HEADS, 0))

    return pl.pallas_call(
        functools.partial(_ret_kernel, n_chunks=n_chunks),
        grid=(batch * HEADS,),
        in_specs=[
            pl.BlockSpec((1, 2, 8, HEAD_DIM), lambda p: (p % HEADS, 0, 0, 0)),
            head_spec(G_Q), head_spec(G_K), head_spec(G_V), head_spec(G_GRET),
            pl.BlockSpec((1, 1, HEAD_DIM), lambda p: (p % HEADS, 0, 0)),
        ],
        out_specs=pl.BlockSpec((1, seq, HEAD_DIM), lambda p: (p % HEADS, p // HEADS, 0)),
        out_shape=jax.ShapeDtypeStruct((HEADS, tokens, HEAD_DIM), BF16),
        scratch_shapes=[pltpu.VMEM((n_chunks, CHUNK, HEAD_DIM), BF16),
                        pltpu.VMEM((n_chunks, CHUNK, HEAD_DIM), BF16),
                        pltpu.VMEM((seq, CHUNK), BF16)],
        compiler_params=pltpu.CompilerParams(
            dimension_semantics=("arbitrary",),
            vmem_limit_bytes=VMEM_LIMIT),
        name="retention",
    )(logits, heads, heads, heads, heads, gn_gain)


def _merge_kernel(rows_ref, uprev_ref, unext_ref, ret_ref, x_ref, convw_ref, wb_ref, wo_ref,
                  fgain_ref, out_ref, *, seq_tiles):
    tm = x_ref.shape[0]
    i = pl.program_id(0)
    pos = i % seq_tiles
    u = rows_ref[R_U].astype(F32)
    prev_row = jnp.where(pos == 0, 0.0, uprev_ref[0, HALO_ROWS - 1:HALO_ROWS, :].astype(F32))
    next_row = jnp.where(pos == seq_tiles - 1, 0.0, unext_ref[0, 0:1, :].astype(F32))
    ridx = lax.broadcasted_iota(jnp.int32, (tm, WIDTH), 0)
    u_prev = jnp.where(ridx == 0, prev_row, pltpu.roll(u, 1, axis=0))
    u_next = jnp.where(ridx == tm - 1, next_row, pltpu.roll(u, tm - 1, axis=0))
    w = convw_ref[...]
    conv = w[0:1, :] * u_prev + w[1:2, :] * u + w[2:3, :] * u_next
    branch_conv = (rows_ref[R_CBG].astype(F32) * conv).astype(BF16)

    def up_project(rs):
        branch_ret = jnp.concatenate([ret_ref[hd, rs, :] for hd in range(HEADS)], axis=1)
        ups = []
        for half in range(D_MODEL // WIDTH):
            cols = slice(half * WIDTH, (half + 1) * WIDTH)
            ups.append((jnp.dot(branch_ret, wb_ref[0, :, cols], preferred_element_type=F32),
                        jnp.dot(branch_conv[rs, :], wb_ref[1, :, cols],
                                preferred_element_type=F32)))
        return ups

    def gated_merge(rs, ups):
        halves = []
        for half, (up_ret, up_conv) in enumerate(ups):
            gate_ret = rows_ref[R_GATE0 + half, rs, :].astype(F32)
            gate_conv = rows_ref[R_GATE0 + D_MODEL // WIDTH + half, rs, :].astype(F32)
            halves.append((gate_ret * up_ret + gate_conv * up_conv).astype(BF16))
        return jnp.concatenate(halves, axis=1)

    def out_project(rs, merged):
        return x_ref[rs, :] + jnp.dot(merged, wo_ref[...], preferred_element_type=F32)

    def normalise(rs, y):
        ms = jnp.mean(y * y, axis=-1, keepdims=True)
        out_ref[rs, :] = y * lax.rsqrt(ms + EPS) * fgain_ref[...]

    blocks = [slice(r0, r0 + MERGE_SUB_ROWS) for r0 in range(0, tm, MERGE_SUB_ROWS)]
    ups = [up_project(rs) for rs in blocks]
    outs = []
    for rs, up in zip(blocks, ups):
        outs.append(out_project(rs, gated_merge(rs, up)))
    for rs, y in zip(blocks, outs):
        normalise(rs, y)


def _merge(rows, ret, x2, conv_w, wb_bf, wo_bf, fgain, seq):
    tokens = x2.shape[0]
    tm = MERGE_ROWS
    seq_tiles = seq // tm
    halo_per_tile = tm // HALO_ROWS
    n_halo = tokens // HALO_ROWS
    resident = pl.Buffered(1)
    return pl.pallas_call(
        functools.partial(_merge_kernel, seq_tiles=seq_tiles),
        grid=(tokens // tm,),
        in_specs=[
            pl.BlockSpec((N_ROW_GROUPS, tm, WIDTH), lambda i: (0, i, 0)),
            pl.BlockSpec((1, HALO_ROWS, WIDTH),
                         lambda i: (R_U, jnp.maximum(i * halo_per_tile - 1, 0), 0)),
            pl.BlockSpec((1, HALO_ROWS, WIDTH),
                         lambda i: (R_U, jnp.minimum((i + 1) * halo_per_tile, n_halo - 1), 0)),
            pl.BlockSpec((HEADS, tm, HEAD_DIM), lambda i: (0, i, 0)),
            pl.BlockSpec((tm, D_MODEL), lambda i: (i, 0)),
            pl.BlockSpec((CONV_K, WIDTH), lambda i: (0, 0)),
            pl.BlockSpec((2, WIDTH, D_MODEL), lambda i: (0, 0, 0), pipeline_mode=resident),
            pl.BlockSpec((D_MODEL, D_MODEL), lambda i: (0, 0), pipeline_mode=resident),
            pl.BlockSpec((1, D_MODEL), lambda i: (0, 0)),
        ],
        out_specs=pl.BlockSpec((tm, D_MODEL), lambda i: (i, 0)),
        out_shape=jax.ShapeDtypeStruct((tokens, D_MODEL), F32),
        compiler_params=pltpu.CompilerParams(
            dimension_semantics=("arbitrary",),
            vmem_limit_bytes=VMEM_LIMIT),
        name="merge",
    )(rows, rows, rows, ret, x2, conv_w, wb_bf, wo_bf, fgain)


def _rotary_tables(seq, tile_rows):
    inv_freq = ROPE_BASE ** (-jnp.arange(0, HEAD_DIM, 2, dtype=F32) / HEAD_DIM)
    inv_freq = jnp.concatenate([inv_freq, inv_freq])
    sign = jnp.where(jnp.arange(HEAD_DIM) < HEAD_DIM // 2, -1.0, 1.0).astype(F32)

    def table(pos):
        ang = pos.astype(F32)[:, None] * inv_freq[None, :]
        return jnp.stack([jnp.cos(ang), jnp.sin(ang), sign * jnp.sin(ang)])

    row_tab = table(jnp.arange(tile_rows))
    tile_tab = jnp.transpose(table(jnp.arange(seq // tile_rows) * tile_rows), (1, 0, 2))
    return row_tab, jnp.pad(tile_tab, ((0, 0), (0, 8 - 3), (0, 0)))


def kernel(x, norm_gain, w_in, decay_logit_fwd, decay_logit_bwd, ret_gn_gain, conv_w, w_branch,
           w_out, final_gain):
    batch, seq, d_model = x.shape
    assert d_model == D_MODEL and norm_gain.shape[0] == 1
    assert w_in.shape == (1, D_MODEL, N_COL_TILES * WIDTH)
    assert seq % PROJ_ROWS == 0 and seq % MERGE_ROWS == 0 and seq % CHUNK == 0
    tokens = batch * seq
    x2 = x.reshape(tokens, D_MODEL)
    row_tab, tile_tab = _rotary_tables(seq, PROJ_ROWS)
    heads, rows = _project(x2, norm_gain[0][None, :], w_in[0].astype(BF16), row_tab, tile_tab, seq)

    logits = jnp.stack([decay_logit_fwd[0], decay_logit_bwd[0]], axis=1).astype(F32)
    logits = jnp.broadcast_to(logits[:, :, None, None], (HEADS, 2, 8, HEAD_DIM))
    gn_gain = ret_gn_gain[0].astype(F32).reshape(HEADS, 1, HEAD_DIM)
    ret = _retention(heads, logits, gn_gain, batch, seq)

    out = _merge(rows, ret, x2, conv_w[0], w_branch[0].astype(BF16), w_out[0].astype(BF16),
                 final_gain[None, :], seq)
    return out.reshape(batch, seq, D_MODEL)
```

```python
import functools

import jax
import jax.numpy as jnp
from jax import lax
from jax.experimental import pallas as pl
from jax.experimental.pallas import tpu as pltpu

F32 = jnp.float32
BF16 = jnp.bfloat16

D_MODEL = 2048
HEADS = 8
HEAD_DIM = 128
WIDTH = HEADS * HEAD_DIM
CHUNK = 128
CONV_K = 3
ROPE_BASE = 10000.0
EPS = 1e-6
G_Q, G_K, G_V, G_GRET, G_CB, G_CC, G_CX, G_GCONV, G_GATE0 = range(9)
N_COL_TILES = 12
N_HEAD_GROUPS = 4
R_U, R_CBG, R_GATE0 = 0, 1, 2
N_ROW_GROUPS = 6

PROJ_ROWS = 1024
PROJ_SUB_ROWS = 256
PROJ_COL_GROUPS = 2
MERGE_ROWS = 512
MERGE_SUB_ROWS = 256
STATE_UNROLL = 64
OUT_UNROLL = 64
HALO_ROWS = 16
VMEM_LIMIT = 58 * 1024 * 1024


def _proj_kernel(x_hbm, gain_ref, w_ref, rowtab_ref, tiletab_ref, heads_ref, rows_ref, x_ref,
                 x_sem, h_ref, cb_ref, cc_ref):
    i = pl.program_id(0)
    j = pl.program_id(1)
    tm = h_ref.shape[0]

    def x_copy(tile):
        return pltpu.make_async_copy(x_hbm.at[pl.ds(tile * tm, tm), :], x_ref, x_sem)

    @pl.when(jnp.logical_and(i == 0, j == 0))
    def _():
        x_copy(0).start()

    @pl.when(jnp.logical_and(j == 1, i + 1 < pl.num_programs(0)))
    def _():
        x_copy(i + 1).start()

    def row_blocks():
        for r0 in range(0, tm, PROJ_SUB_ROWS):
            yield slice(r0, r0 + PROJ_SUB_ROWS)

    def project(h, group):
        return jnp.dot(h, w_ref[:, group * WIDTH:(group + 1) * WIDTH],
                       preferred_element_type=F32)

    def store_heads(group, rs, r):
        for hd in range(HEADS):
            heads_ref[group * HEADS + hd, rs, :] = (
                r[:, hd * HEAD_DIM:(hd + 1) * HEAD_DIM].astype(BF16))

    def store_rotary(group, rs, r, scale):
        tile_cos, tile_sin, tile_ssin = (tiletab_ref[0, t:t + 1, :] for t in range(3))
        cos = tile_cos * rowtab_ref[0, rs, :] - tile_sin * rowtab_ref[1, rs, :]
        sin = tile_ssin * rowtab_ref[0, rs, :] + tile_cos * rowtab_ref[2, rs, :]
        for hd in range(HEADS):
            rh = r[:, hd * HEAD_DIM:(hd + 1) * HEAD_DIM]
            o = rh * cos + pltpu.roll(rh, HEAD_DIM // 2, axis=1) * sin
            if scale is not None:
                o = o * scale
            heads_ref[group * HEADS + hd, rs, :] = o.astype(BF16)

    @pl.when(j == 0)
    def _():
        x_copy(i).wait()
        for rs in row_blocks():
            x = x_ref[rs, :]
            ms = jnp.mean(x * x, axis=-1, keepdims=True)
            h = (x * lax.rsqrt(ms + EPS) * gain_ref[...]).astype(BF16)
            h_ref[rs, :] = h
            store_rotary(0, rs, project(h, 0), None)
            store_rotary(1, rs, project(h, 1), HEAD_DIM ** -0.5)

    @pl.when(j == 1)
    def _():
        for rs in row_blocks():
            h = h_ref[rs, :]
            store_heads(0, rs, project(h, 0))
            r = project(h, 1)
            store_heads(1, rs, r * jax.nn.sigmoid(r))

    @pl.when(j == 2)
    def _():
        for rs in row_blocks():
            h = h_ref[rs, :]
            cb_ref[rs, :] = project(h, 0).astype(BF16)
            cc_ref[rs, :] = project(h, 1).astype(BF16)

    @pl.when(j == 3)
    def _():
        for rs in row_blocks():
            h = h_ref[rs, :]
            rows_ref[0, rs, :] = (cc_ref[rs, :].astype(F32) * project(h, 0)).astype(BF16)
            r = project(h, 1)
            rows_ref[1, rs, :] = (cb_ref[rs, :].astype(F32) * (r * jax.nn.sigmoid(r))
                                  ).astype(BF16)

    @pl.when(j >= 4)
    def _():
        for rs in row_blocks():
            h = h_ref[rs, :]
            rows_ref[0, rs, :] = jax.nn.sigmoid(project(h, 0)).astype(BF16)
            rows_ref[1, rs, :] = jax.nn.sigmoid(project(h, 1)).astype(BF16)


def _project(x2, gain, w_bf, row_tab, tile_tab, seq):
    tokens = x2.shape[0]
    tm = PROJ_ROWS
    seq_tiles = seq // tm
    grid = (tokens // tm, N_COL_TILES // PROJ_COL_GROUPS)
    return pl.pallas_call(
        _proj_kernel,
        grid=grid,
        in_specs=[
            pl.BlockSpec(memory_space=pl.ANY),
            pl.BlockSpec((1, D_MODEL), lambda i, j: (0, 0)),
            pl.BlockSpec((D_MODEL, PROJ_COL_GROUPS * WIDTH), lambda i, j: (0, j)),
            pl.BlockSpec((3, tm, HEAD_DIM), lambda i, j: (0, 0, 0)),
            pl.BlockSpec((1, 8, HEAD_DIM), lambda i, j: (i % seq_tiles, 0, 0)),
        ],
        out_specs=[
            pl.BlockSpec((PROJ_COL_GROUPS * HEADS, tm, HEAD_DIM),
                         lambda i, j: (jnp.minimum(j, 1), i, 0)),
            pl.BlockSpec((PROJ_COL_GROUPS, tm, WIDTH),
                         lambda i, j: (jnp.maximum(j - 3, 0), i, 0)),
        ],
        out_shape=[
            jax.ShapeDtypeStruct((N_HEAD_GROUPS * HEADS, tokens, HEAD_DIM), BF16),
            jax.ShapeDtypeStruct((N_ROW_GROUPS, tokens, WIDTH), BF16),
        ],
        scratch_shapes=[
            pltpu.VMEM((tm, D_MODEL), F32),
            pltpu.SemaphoreType.DMA(()),
            pltpu.VMEM((tm, D_MODEL), BF16),
            pltpu.VMEM((tm, WIDTH), BF16),
            pltpu.VMEM((tm, WIDTH), BF16),
        ],
        compiler_params=pltpu.CompilerParams(
            dimension_semantics=("arbitrary", "arbitrary"),
            vmem_limit_bytes=VMEM_LIMIT),
        name="proj",
    )(x2, gain, w_bf, row_tab, tile_tab)


def _log_sigmoid(z):
    return -(jnp.maximum(-z, 0.0) + jnp.log1p(jnp.exp(-jnp.abs(z))))


def _ret_kernel(logit_ref, q_ref, k_ref, v_ref, g_ref, gain_ref, o_ref, sf_ref, sb_ref, p_ref, *,
                n_chunks):
    lf = _log_sigmoid(logit_ref[0, 0])[0:1, :]
    lb = _log_sigmoid(logit_ref[0, 1])[0:1, :]
    row = lax.broadcasted_iota(jnp.int32, (CHUNK, CHUNK), 0).astype(F32)
    col = lax.broadcasted_iota(jnp.int32, (CHUNK, CHUNK), 1).astype(F32)
    dist = row - col
    decay = jnp.exp(jnp.where(dist >= 0, dist * lf, -dist * lb))
    wq_f = jnp.exp(row * lf).astype(BF16)
    wq_b = jnp.exp((CHUNK - 1 - row) * lb).astype(BF16)
    wk_f = jnp.exp((CHUNK - row) * lf).astype(BF16)
    wk_b = jnp.exp((row + 1) * lb).astype(BF16)
    chunk_f = jnp.exp(CHUNK * lf)
    chunk_b = jnp.exp(CHUNK * lb)
    gain = gain_ref[0]

    def rows_of(c):
        return pl.ds(pl.multiple_of(c * CHUNK, CHUNK), CHUNK)

    def kv_summary(k_weighted, v):
        return lax.dot_general(k_weighted, v, (((0,), (0,)), ((), ())),
                               preferred_element_type=F32)

    def state_body(i, carry):
        state_f, state_b = carry
        cf = i
        cb = n_chunks - 1 - i
        sf_ref[cf] = state_f.astype(BF16)
        sb_ref[cb] = state_b.astype(BF16)
        rows_f = rows_of(cf)
        rows_b = rows_of(cb)
        k_f = k_ref[0, rows_f, :]
        scores = lax.dot_general(q_ref[0, rows_f, :], k_f, (((1,), (1,)), ((), ())),
                                 preferred_element_type=F32)
        p_ref[rows_f, :] = (scores * decay).astype(BF16)
        return (chunk_f * state_f + kv_summary(k_f * wk_f, v_ref[0, rows_f, :]),
                chunk_b * state_b + kv_summary(k_ref[0, rows_b, :] * wk_b, v_ref[0, rows_b, :]))

    zero = jnp.zeros((CHUNK, HEAD_DIM), F32)
    lax.fori_loop(0, n_chunks, state_body, (zero, zero), unroll=STATE_UNROLL)

    def out_body(c, carry):
        rows = rows_of(c)
        q = q_ref[0, rows, :]
        lhs = jnp.concatenate([p_ref[rows, :], q * wq_f, q * wq_b], axis=1)
        rhs = jnp.concatenate([v_ref[0, rows, :], sf_ref[c], sb_ref[c]], axis=0)
        o = jnp.dot(lhs, rhs, preferred_element_type=F32)
        ms = jnp.mean(o * o, axis=-1, keepdims=True)
        o = o * lax.rsqrt(ms + EPS) * gain * g_ref[0, rows, :].astype(F32)
        o_ref[0, rows, :] = o.astype(BF16)
        return carry

    lax.fori_loop(0, n_chunks, out_body, 0, unroll=OUT_UNROLL)


def _retention(heads, logits, gn_gain, batch, seq):
    tokens = batch * seq
    n_chunks = seq // CHUNK

    def head_spec(group):
        return pl.BlockSpec((1, seq, HEAD_DIM),
                            lambda p: (group * HEADS + p % HEADS, p // HEADS, 0))

    return pl.pallas_call(
        functools.partial(_ret_kernel, n_chunks=n_chunks),
        grid=(batch * HEADS,),
        in_specs=[
            pl.BlockSpec((1, 2, 8, HEAD_DIM), lambda p: (p % HEADS, 0, 0, 0)),
            head_spec(G_Q), head_spec(G_K), head_spec(G_V), head_spec(G_GRET),
            pl.BlockSpec((1, 1, HEAD_DIM), lambda p: (p % HEADS, 0, 0)),
        ],
        out_specs=pl.BlockSpec((1, seq, HEAD_DIM), lambda p: (p % HEADS, p // HEADS, 0)),
        out_shape=jax.ShapeDtypeStruct((HEADS, tokens, HEAD_DIM), BF16),
        scratch_shapes=[pltpu.VMEM((n_chunks, CHUNK, HEAD_DIM), BF16),
                        pltpu.VMEM((n_chunks, CHUNK, HEAD_DIM), BF16),
                        pltpu.VMEM((seq, CHUNK), BF16)],
        compiler_params=pltpu.CompilerParams(
            dimension_semantics=("arbitrary",),
            vmem_limit_bytes=VMEM_LIMIT),
        name="retention",
    )(logits, heads, heads, heads, heads, gn_gain)


def _merge_kernel(rows_ref, uprev_ref, unext_ref, ret_ref, x_ref, convw_ref, wb_ref, wo_ref,
                  fgain_ref, out_ref, *, seq_tiles):
    tm = x_ref.shape[0]
    i = pl.program_id(0)
    pos = i % seq_tiles
    u = rows_ref[R_U].astype(F32)
    prev_row = jnp.where(pos == 0, 0.0, uprev_ref[0, HALO_ROWS - 1:HALO_ROWS, :].astype(F32))
    next_row = jnp.where(pos == seq_tiles - 1, 0.0, unext_ref[0, 0:1, :].astype(F32))
    ridx = lax.broadcasted_iota(jnp.int32, (tm, WIDTH), 0)
    u_prev = jnp.where(ridx == 0, prev_row, pltpu.roll(u, 1, axis=0))
    u_next = jnp.where(ridx == tm - 1, next_row, pltpu.roll(u, tm - 1, axis=0))
    w = convw_ref[...]
    conv = w[0:1, :] * u_prev + w[1:2, :] * u + w[2:3, :] * u_next
    branch_conv = (rows_ref[R_CBG].astype(F32) * conv).astype(BF16)

    def up_project(rs):
        branch_ret = jnp.concatenate([ret_ref[hd, rs, :] for hd in range(HEADS)], axis=1)
        ups = []
        for half in range(D_MODEL // WIDTH):
            cols = slice(half * WIDTH, (half + 1) * WIDTH)
            ups.append((jnp.dot(branch_ret, wb_ref[0, :, cols], preferred_element_type=F32),
                        jnp.dot(branch_conv[rs, :], wb_ref[1, :, cols],
                                preferred_element_type=F32)))
        return ups

    def gated_merge(rs, ups):
        halves = []
        for half, (up_ret, up_conv) in enumerate(ups):
            gate_ret = rows_ref[R_GATE0 + half, rs, :].astype(F32)
            gate_conv = rows_ref[R_GATE0 + D_MODEL // WIDTH + half, rs, :].astype(F32)
            halves.append((gate_ret * up_ret + gate_conv * up_conv).astype(BF16))
        return jnp.concatenate(halves, axis=1)

    def out_project(rs, merged):
        return x_ref[rs, :] + jnp.dot(merged, wo_ref[...], preferred_element_type=F32)

    def normalise(rs, y):
        ms = jnp.mean(y * y, axis=-1, keepdims=True)
        out_ref[rs, :] = y * lax.rsqrt(ms + EPS) * fgain_ref[...]

    blocks = [slice(r0, r0 + MERGE_SUB_ROWS) for r0 in range(0, tm, MERGE_SUB_ROWS)]
    ups = [up_project(rs) for rs in blocks]
    outs = []
    for rs, up in zip(blocks, ups):
        outs.append(out_project(rs, gated_merge(rs, up)))
    for rs, y in zip(blocks, outs):
        normalise(rs, y)


def _merge(rows, ret, x2, conv_w, wb_bf, wo_bf, fgain, seq):
    tokens = x2.shape[0]
    tm = MERGE_ROWS
    seq_tiles = seq // tm
    halo_per_tile = tm // HALO_ROWS
    n_halo = tokens // HALO_ROWS
    resident = pl.Buffered(1)
    return pl.pallas_call(
        functools.partial(_merge_kernel, seq_tiles=seq_tiles),
        grid=(tokens // tm,),
        in_specs=[
            pl.BlockSpec((N_ROW_GROUPS, tm, WIDTH), lambda i: (0, i, 0)),
            pl.BlockSpec((1, HALO_ROWS, WIDTH),
                         lambda i: (R_U, jnp.maximum(i * halo_per_tile - 1, 0), 0)),
            pl.BlockSpec((1, HALO_ROWS, WIDTH),
                         lambda i: (R_U, jnp.minimum((i + 1) * halo_per_tile, n_halo - 1), 0)),
            pl.BlockSpec((HEADS, tm, HEAD_DIM), lambda i: (0, i, 0)),
            pl.BlockSpec((tm, D_MODEL), lambda i: (i, 0)),
            pl.BlockSpec((CONV_K, WIDTH), lambda i: (0, 0)),
            pl.BlockSpec((2, WIDTH, D_MODEL), lambda i: (0, 0, 0), pipeline_mode=resident),
            pl.BlockSpec((D_MODEL, D_MODEL), lambda i: (0, 0), pipeline_mode=resident),
            pl.BlockSpec((1, D_MODEL), lambda i: (0, 0)),
        ],
        out_specs=pl.BlockSpec((tm, D_MODEL), lambda i: (i, 0)),
        out_shape=jax.ShapeDtypeStruct((tokens, D_MODEL), F32),
        compiler_params=pltpu.CompilerParams(
            dimension_semantics=("arbitrary",),
            vmem_limit_bytes=VMEM_LIMIT),
        name="merge",
    )(rows, rows, rows, ret, x2, conv_w, wb_bf, wo_bf, fgain)


def _rotary_tables(seq, tile_rows):
    inv_freq = ROPE_BASE ** (-jnp.arange(0, HEAD_DIM, 2, dtype=F32) / HEAD_DIM)
    inv_freq = jnp.concatenate([inv_freq, inv_freq])
    sign = jnp.where(jnp.arange(HEAD_DIM) < HEAD_DIM // 2, -1.0, 1.0).astype(F32)

    def table(pos):
        ang = pos.astype(F32)[:, None] * inv_freq[None, :]
        return jnp.stack([jnp.cos(ang), jnp.sin(ang), sign * jnp.sin(ang)])

    row_tab = table(jnp.arange(tile_rows))
    tile_tab = jnp.transpose(table(jnp.arange(seq // tile_rows) * tile_rows), (1, 0, 2))
    return row_tab, jnp.pad(tile_tab, ((0, 0), (0, 8 - 3), (0, 0)))


def kernel(x, norm_gain, w_in, decay_logit_fwd, decay_logit_bwd, ret_gn_gain, conv_w, w_branch,
           w_out, final_gain):
    batch, seq, d_model = x.shape
    assert d_model == D_MODEL and norm_gain.shape[0] == 1
    assert w_in.shape == (1, D_MODEL, N_COL_TILES * WIDTH)
    assert seq % PROJ_ROWS == 0 and seq % MERGE_ROWS == 0 and seq % CHUNK == 0
    tokens = batch * seq
    x2 = x.reshape(tokens, D_MODEL)
    row_tab, tile_tab = _rotary_tables(seq, PROJ_ROWS)
    heads, rows = _project(x2, norm_gain[0][None, :], w_in[0].astype(BF16), row_tab, tile_tab, seq)

    logits = jnp.stack([decay_logit_fwd[0], decay_logit_bwd[0]], axis=1).astype(F32)
    logits = jnp.broadcast_to(logits[:, :, None, None], (HEADS, 2, 8, HEAD_DIM))
    gn_gain = ret_gn_gain[0].astype(F32).reshape(HEADS, 1, HEAD_DIM)
    ret = _retention(heads, logits, gn_gain, batch, seq)

    out = _merge(rows, ret, x2, conv_w[0], w_branch[0].astype(BF16), w_out[0].astype(BF16),
                 final_gain[None, :], seq)
    return out.reshape(batch, seq, D_MODEL)
```

```python
import functools

import jax
import jax.numpy as jnp
from jax import lax
from jax.experimental import pallas as pl
from jax.experimental.pallas import tpu as pltpu

F32 = jnp.float32
BF16 = jnp.bfloat16

D_MODEL = 2048
HEADS = 8
HEAD_DIM = 128
WIDTH = HEADS * HEAD_DIM
CHUNK = 128
CONV_K = 3
ROPE_BASE = 10000.0
EPS = 1e-6
G_Q, G_K, G_V, G_GRET, G_CB, G_CC, G_CX, G_GCONV, G_GATE0 = range(9)
N_COL_TILES = 12
N_HEAD_GROUPS = 4
R_U, R_CBG, R_GATE0 = 0, 1, 2
N_ROW_GROUPS = 6
S_CB, S_CC = 0, 1

PROJ_ROWS = 1024
PROJ_SUB_ROWS = 256
MERGE_ROWS = 512
MERGE_SUB_ROWS = 256
STATE_UNROLL = 64
OUT_UNROLL = 64
HALO_ROWS = 16
VMEM_LIMIT = 58 * 1024 * 1024


def _proj_kernel(x_ref, gain_ref, w_ref, rowtab_ref, tiletab_ref, heads_ref, rows_ref, h_ref,
                 stash_ref):
    j = pl.program_id(1)

    def row_blocks(normalize=False):
        for r0 in range(0, h_ref.shape[0], PROJ_SUB_ROWS):
            rs = slice(r0, r0 + PROJ_SUB_ROWS)
            if normalize:
                x = x_ref[rs, :]
                ms = jnp.mean(x * x, axis=-1, keepdims=True)
                h = (x * lax.rsqrt(ms + EPS) * gain_ref[...]).astype(BF16)
                h_ref[rs, :] = h
            else:
                h = h_ref[rs, :]
            yield rs, jnp.dot(h, w_ref[...], preferred_element_type=F32)

    def store_heads(rs, r):
        for hd in range(HEADS):
            heads_ref[hd, rs, :] = r[:, hd * HEAD_DIM:(hd + 1) * HEAD_DIM].astype(BF16)

    def store_rotary(rs, r, scale):
        tile_cos, tile_sin, tile_ssin = (tiletab_ref[0, t:t + 1, :] for t in range(3))
        cos = tile_cos * rowtab_ref[0, rs, :] - tile_sin * rowtab_ref[1, rs, :]
        sin = tile_ssin * rowtab_ref[0, rs, :] + tile_cos * rowtab_ref[2, rs, :]
        for hd in range(HEADS):
            rh = r[:, hd * HEAD_DIM:(hd + 1) * HEAD_DIM]
            o = rh * cos + pltpu.roll(rh, HEAD_DIM // 2, axis=1) * sin
            if scale is not None:
                o = o * scale
            heads_ref[hd, rs, :] = o.astype(BF16)

    @pl.when(j == G_Q)
    def _():
        for rs, r in row_blocks(normalize=True):
            store_rotary(rs, r, None)

    @pl.when(j == G_K)
    def _():
        for rs, r in row_blocks():
            store_rotary(rs, r, HEAD_DIM ** -0.5)

    @pl.when(j == G_V)
    def _():
        for rs, r in row_blocks():
            store_heads(rs, r)

    @pl.when(j == G_GRET)
    def _():
        for rs, r in row_blocks():
            store_heads(rs, r * jax.nn.sigmoid(r))

    @pl.when(jnp.logical_or(j == G_CB, j == G_CC))
    def _():
        for rs, r in row_blocks():
            stash_ref[j - G_CB, rs, :] = r.astype(BF16)

    @pl.when(j == G_CX)
    def _():
        for rs, r in row_blocks():
            rows_ref[0, rs, :] = (stash_ref[S_CC, rs, :].astype(F32) * r).astype(BF16)

    @pl.when(j == G_GCONV)
    def _():
        for rs, r in row_blocks():
            rows_ref[0, rs, :] = (stash_ref[S_CB, rs, :].astype(F32) * (r * jax.nn.sigmoid(r))
                                  ).astype(BF16)

    @pl.when(j >= G_GATE0)
    def _():
        for rs, r in row_blocks():
            rows_ref[0, rs, :] = jax.nn.sigmoid(r).astype(BF16)


def _project(x2, gain, w_bf, row_tab, tile_tab, seq):
    tokens = x2.shape[0]
    tm = PROJ_ROWS
    seq_tiles = seq // tm
    grid = (tokens // tm, N_COL_TILES)
    return pl.pallas_call(
        _proj_kernel,
        grid=grid,
        in_specs=[
            pl.BlockSpec((tm, D_MODEL), lambda i, j: (i, 0)),
            pl.BlockSpec((1, D_MODEL), lambda i, j: (0, 0)),
            pl.BlockSpec((D_MODEL, WIDTH), lambda i, j: (0, j)),
            pl.BlockSpec((3, tm, HEAD_DIM), lambda i, j: (0, 0, 0)),
            pl.BlockSpec((1, 8, HEAD_DIM), lambda i, j: (i % seq_tiles, 0, 0)),
        ],
        out_specs=[
            pl.BlockSpec((HEADS, tm, HEAD_DIM),
                         lambda i, j: (jnp.minimum(j, N_HEAD_GROUPS - 1), i, 0)),
            pl.BlockSpec((1, tm, WIDTH),
                         lambda i, j: (jnp.maximum(j - G_CX, 0), i, 0)),
        ],
        out_shape=[
            jax.ShapeDtypeStruct((N_HEAD_GROUPS * HEADS, tokens, HEAD_DIM), BF16),
            jax.ShapeDtypeStruct((N_ROW_GROUPS, tokens, WIDTH), BF16),
        ],
        scratch_shapes=[
            pltpu.VMEM((tm, D_MODEL), BF16),
            pltpu.VMEM((2, tm, WIDTH), BF16),
        ],
        compiler_params=pltpu.CompilerParams(
            dimension_semantics=("arbitrary", "arbitrary"),
            vmem_limit_bytes=VMEM_LIMIT),
        name="proj",
    )(x2, gain, w_bf, row_tab, tile_tab)


def _log_sigmoid(z):
    return -(jnp.maximum(-z, 0.0) + jnp.log1p(jnp.exp(-jnp.abs(z))))


def _ret_kernel(logit_ref, q_ref, k_ref, v_ref, g_ref, gain_ref, o_ref, sf_ref, sb_ref, p_ref, *,
                n_chunks):
    lf = _log_sigmoid(logit_ref[0, 0])[0:1, :]
    lb = _log_sigmoid(logit_ref[0, 1])[0:1, :]
    row = lax.broadcasted_iota(jnp.int32, (CHUNK, CHUNK), 0).astype(F32)
    col = lax.broadcasted_iota(jnp.int32, (CHUNK, CHUNK), 1).astype(F32)
    dist = row - col
    decay = jnp.exp(jnp.where(dist >= 0, dist * lf, -dist * lb))
    wq_f = jnp.exp(row * lf).astype(BF16)
    wq_b = jnp.exp((CHUNK - 1 - row) * lb).astype(BF16)
    wk_f = jnp.exp((CHUNK - row) * lf).astype(BF16)
    wk_b = jnp.exp((row + 1) * lb).astype(BF16)
    chunk_f = jnp.exp(CHUNK * lf)
    chunk_b = jnp.exp(CHUNK * lb)
    gain = gain_ref[0]

    def rows_of(c):
        return pl.ds(pl.multiple_of(c * CHUNK, CHUNK), CHUNK)

    def kv_summary(k_weighted, v):
        return lax.dot_general(k_weighted, v, (((0,), (0,)), ((), ())),
                               preferred_element_type=F32)

    def state_body(i, carry):
        state_f, state_b = carry
        cf = i
        cb = n_chunks - 1 - i
        sf_ref[cf] = state_f.astype(BF16)
        sb_ref[cb] = state_b.astype(BF16)
        rows_f = rows_of(cf)
        rows_b = rows_of(cb)
        k_f = k_ref[0, rows_f, :]
        scores = lax.dot_general(q_ref[0, rows_f, :], k_f, (((1,), (1,)), ((), ())),
                                 preferred_element_type=F32)
        p_ref[rows_f, :] = (scores * decay).astype(BF16)
        return (chunk_f * state_f + kv_summary(k_f * wk_f, v_ref[0, rows_f, :]),
                chunk_b * state_b + kv_summary(k_ref[0, rows_b, :] * wk_b, v_ref[0, rows_b, :]))

    zero = jnp.zeros((CHUNK, HEAD_DIM), F32)
    lax.fori_loop(0, n_chunks, state_body, (zero, zero), unroll=STATE_UNROLL)

    def out_body(c, carry):
        rows = rows_of(c)
        q = q_ref[0, rows, :]
        lhs = jnp.concatenate([p_ref[rows, :], q * wq_f, q * wq_b], axis=1)
        rhs = jnp.concatenate([v_ref[0, rows, :], sf_ref[c], sb_ref[c]], axis=0)
        o = jnp.dot(lhs, rhs, preferred_element_type=F32)
        ms = jnp.mean(o * o, axis=-1, keepdims=True)
        o = o * lax.rsqrt(ms + EPS) * gain * g_ref[0, rows, :].astype(F32)
        o_ref[0, rows, :] = o.astype(BF16)
        return carry

    lax.fori_loop(0, n_chunks, out_body, 0, unroll=OUT_UNROLL)


def _retention(heads, logits, gn_gain, batch, seq):
    tokens = batch * seq
    n_chunks = seq // CHUNK

    def head_spec(group):
        return pl.BlockSpec((1, seq, HEAD_DIM),
                            lambda p: (group * HEADS + p % HEADS, p // HEADS, 0))

    return pl.pallas_call(
        functools.partial(_ret_kernel, n_chunks=n_chunks),
        grid=(batch * HEADS,),
        in_specs=[
            pl.BlockSpec((1, 2, 8, HEAD_DIM), lambda p: (p % HEADS, 0, 0, 0)),
            head_spec(G_Q), head_spec(G_K), head_spec(G_V), head_spec(G_GRET),
            pl.BlockSpec((1, 1, HEAD_DIM), lambda p: (p % HEADS, 0, 0)),
        ],
        out_specs=pl.BlockSpec((1, seq, HEAD_DIM), lambda p: (p % HEADS, p // HEADS, 0)),
        out_shape=jax.ShapeDtypeStruct((HEADS, tokens, HEAD_DIM), BF16),
        scratch_shapes=[pltpu.VMEM((n_chunks, CHUNK, HEAD_DIM), BF16),
                        pltpu.VMEM((n_chunks, CHUNK, HEAD_DIM), BF16),
                        pltpu.VMEM((seq, CHUNK), BF16)],
        compiler_params=pltpu.CompilerParams(
            dimension_semantics=("arbitrary",),
            vmem_limit_bytes=VMEM_LIMIT),
        name="retention",
    )(logits, heads, heads, heads, heads, gn_gain)


def _merge_kernel(rows_ref, uprev_ref, unext_ref, ret_ref, x_ref, convw_ref, wb_ref, wo_ref,
                  fgain_ref, out_ref, *, seq_tiles):
    tm = x_ref.shape[0]
    i = pl.program_id(0)
    pos = i % seq_tiles
    u = rows_ref[R_U].astype(F32)
    prev_row = jnp.where(pos == 0, 0.0, uprev_ref[0, HALO_ROWS - 1:HALO_ROWS, :].astype(F32))
    next_row = jnp.where(pos == seq_tiles - 1, 0.0, unext_ref[0, 0:1, :].astype(F32))
    ridx = lax.broadcasted_iota(jnp.int32, (tm, WIDTH), 0)
    u_prev = jnp.where(ridx == 0, prev_row, pltpu.roll(u, 1, axis=0))
    u_next = jnp.where(ridx == tm - 1, next_row, pltpu.roll(u, tm - 1, axis=0))
    w = convw_ref[...]
    conv = w[0:1, :] * u_prev + w[1:2, :] * u + w[2:3, :] * u_next
    branch_conv = (rows_ref[R_CBG].astype(F32) * conv).astype(BF16)

    def up_project(rs):
        branch_ret = jnp.concatenate([ret_ref[hd, rs, :] for hd in range(HEADS)], axis=1)
        ups = []
        for half in range(D_MODEL // WIDTH):
            cols = slice(half * WIDTH, (half + 1) * WIDTH)
            ups.append((jnp.dot(branch_ret, wb_ref[0, :, cols], preferred_element_type=F32),
                        jnp.dot(branch_conv[rs, :], wb_ref[1, :, cols],
                                preferred_element_type=F32)))
        return ups

    def gated_merge(rs, ups):
        halves = []
        for half, (up_ret, up_conv) in enumerate(ups):
            gate_ret = rows_ref[R_GATE0 + half, rs, :].astype(F32)
            gate_conv = rows_ref[R_GATE0 + D_MODEL // WIDTH + half, rs, :].astype(F32)
            halves.append((gate_ret * up_ret + gate_conv * up_conv).astype(BF16))
        return jnp.concatenate(halves, axis=1)

    def out_project(rs, merged):
        return x_ref[rs, :] + jnp.dot(merged, wo_ref[...], preferred_element_type=F32)

    def normalise(rs, y):
        ms = jnp.mean(y * y, axis=-1, keepdims=True)
        out_ref[rs, :] = y * lax.rsqrt(ms + EPS) * fgain_ref[...]

    blocks = [slice(r0, r0 + MERGE_SUB_ROWS) for r0 in range(0, tm, MERGE_SUB_ROWS)]
    ups = [up_project(rs) for rs in blocks]
    outs = []
    for rs, up in zip(blocks, ups):
        outs.append(out_project(rs, gated_merge(rs, up)))
    for rs, y in zip(blocks, outs):
        normalise(rs, y)


def _merge(rows, ret, x2, conv_w, wb_bf, wo_bf, fgain, seq):
    tokens = x2.shape[0]
    tm = MERGE_ROWS
    seq_tiles = seq // tm
    halo_per_tile = tm // HALO_ROWS
    n_halo = tokens // HALO_ROWS
    resident = pl.Buffered(1)
    return pl.pallas_call(
        functools.partial(_merge_kernel, seq_tiles=seq_tiles),
        grid=(tokens // tm,),
        in_specs=[
            pl.BlockSpec((N_ROW_GROUPS, tm, WIDTH), lambda i: (0, i, 0)),
            pl.BlockSpec((1, HALO_ROWS, WIDTH),
                         lambda i: (R_U, jnp.maximum(i * halo_per_tile - 1, 0), 0)),
            pl.BlockSpec((1, HALO_ROWS, WIDTH),
                         lambda i: (R_U, jnp.minimum((i + 1) * halo_per_tile, n_halo - 1), 0)),
            pl.BlockSpec((HEADS, tm, HEAD_DIM), lambda i: (0, i, 0)),
            pl.BlockSpec((tm, D_MODEL), lambda i: (i, 0)),
            pl.BlockSpec((CONV_K, WIDTH), lambda i: (0, 0)),
            pl.BlockSpec((2, WIDTH, D_MODEL), lambda i: (0, 0, 0), pipeline_mode=resident),
            pl.BlockSpec((D_MODEL, D_MODEL), lambda i: (0, 0), pipeline_mode=resident),
            pl.BlockSpec((1, D_MODEL), lambda i: (0, 0)),
        ],
        out_specs=pl.BlockSpec((tm, D_MODEL), lambda i: (i, 0)),
        out_shape=jax.ShapeDtypeStruct((tokens, D_MODEL), F32),
        compiler_params=pltpu.CompilerParams(
            dimension_semantics=("arbitrary",),
            vmem_limit_bytes=VMEM_LIMIT),
        name="merge",
    )(rows, rows, rows, ret, x2, conv_w, wb_bf, wo_bf, fgain)


def _rotary_tables(seq, tile_rows):
    inv_freq = ROPE_BASE ** (-jnp.arange(0, HEAD_DIM, 2, dtype=F32) / HEAD_DIM)
    inv_freq = jnp.concatenate([inv_freq, inv_freq])
    sign = jnp.where(jnp.arange(HEAD_DIM) < HEAD_DIM // 2, -1.0, 1.0).astype(F32)

    def table(pos):
        ang = pos.astype(F32)[:, None] * inv_freq[None, :]
        return jnp.stack([jnp.cos(ang), jnp.sin(ang), sign * jnp.sin(ang)])

    row_tab = table(jnp.arange(tile_rows))
    tile_tab = jnp.transpose(table(jnp.arange(seq // tile_rows) * tile_rows), (1, 0, 2))
    return row_tab, jnp.pad(tile_tab, ((0, 0), (0, 8 - 3), (0, 0)))


def kernel(x, norm_gain, w_in, decay_logit_fwd, decay_logit_bwd, ret_gn_gain, conv_w, w_branch,
           w_out, final_gain):
    batch, seq, d_model = x.shape
    assert d_model == D_MODEL and norm_gain.shape[0] == 1
    assert w_in.shape == (1, D_MODEL, N_COL_TILES * WIDTH)
    assert seq % PROJ_ROWS == 0 and seq % MERGE_ROWS == 0 and seq % CHUNK == 0
    tokens = batch * seq
    x2 = x.reshape(tokens, D_MODEL)
    row_tab, tile_tab = _rotary_tables(seq, PROJ_ROWS)
    heads, rows = _project(x2, norm_gain[0][None, :], w_in[0].astype(BF16), row_tab, tile_tab, seq)

    logits = jnp.stack([decay_logit_fwd[0], decay_logit_bwd[0]], axis=1).astype(F32)
    logits = jnp.broadcast_to(logits[:, :, None, None], (HEADS, 2, 8, HEAD_DIM))
    gn_gain = ret_gn_gain[0].astype(F32).reshape(HEADS, 1, HEAD_DIM)
    ret = _retention(heads, logits, gn_gain, batch, seq)

    out = _merge(rows, ret, x2, conv_w[0], w_branch[0].astype(BF16), w_out[0].astype(BF16),
                 final_gain[None, :], seq)
    return out.reshape(batch, seq, D_MODEL)
```

```python
import functools

import jax
import jax.numpy as jnp
from jax import lax
from jax.experimental import pallas as pl
from jax.experimental.pallas import tpu as pltpu

F32 = jnp.float32
BF16 = jnp.bfloat16

D_MODEL = 2048
HEADS = 8
HEAD_DIM = 128
WIDTH = HEADS * HEAD_DIM
CHUNK = 128
CONV_K = 3
ROPE_BASE = 10000.0
EPS = 1e-6
G_Q, G_K, G_V, G_GRET, G_CB, G_CC, G_CX, G_GCONV, G_GATE0 = range(9)
N_COL_TILES = 12
N_HEAD_GROUPS = 4
R_U, R_CBG, R_GATE0 = 0, 1, 2
N_ROW_GROUPS = 6
S_CB, S_CC = 0, 1

PROJ_ROWS = 1024
PROJ_SUB_ROWS = 256
MERGE_ROWS = 512
MERGE_SUB_ROWS = 256
STATE_UNROLL = 64
OUT_UNROLL = 64
HALO_ROWS = 16
VMEM_LIMIT = 58 * 1024 * 1024


def _proj_kernel(x_hbm, gain_ref, w_ref, rowtab_ref, tiletab_ref, heads_ref, rows_ref, x_ref,
                 x_sem, h_ref, stash_ref):
    i = pl.program_id(0)
    j = pl.program_id(1)
    tm = h_ref.shape[0]

    def x_copy(tile):
        return pltpu.make_async_copy(x_hbm.at[pl.ds(tile * tm, tm), :], x_ref, x_sem)

    @pl.when(jnp.logical_and(i == 0, j == 0))
    def _():
        x_copy(0).start()

    @pl.when(jnp.logical_and(j == 1, i + 1 < pl.num_programs(0)))
    def _():
        x_copy(i + 1).start()

    @pl.when(j == 0)
    def _():
        x_copy(i).wait()

    def row_blocks(normalize=False):
        for r0 in range(0, h_ref.shape[0], PROJ_SUB_ROWS):
            rs = slice(r0, r0 + PROJ_SUB_ROWS)
            if normalize:
                x = x_ref[rs, :]
                ms = jnp.mean(x * x, axis=-1, keepdims=True)
                h = (x * lax.rsqrt(ms + EPS) * gain_ref[...]).astype(BF16)
                h_ref[rs, :] = h
            else:
                h = h_ref[rs, :]
            yield rs, jnp.dot(h, w_ref[...], preferred_element_type=F32)

    def store_heads(rs, r):
        for hd in range(HEADS):
            heads_ref[hd, rs, :] = r[:, hd * HEAD_DIM:(hd + 1) * HEAD_DIM].astype(BF16)

    def store_rotary(rs, r, scale):
        tile_cos, tile_sin, tile_ssin = (tiletab_ref[0, t:t + 1, :] for t in range(3))
        cos = tile_cos * rowtab_ref[0, rs, :] - tile_sin * rowtab_ref[1, rs, :]
        sin = tile_ssin * rowtab_ref[0, rs, :] + tile_cos * rowtab_ref[2, rs, :]
        for hd in range(HEADS):
            rh = r[:, hd * HEAD_DIM:(hd + 1) * HEAD_DIM]
            o = rh * cos + pltpu.roll(rh, HEAD_DIM // 2, axis=1) * sin
            if scale is not None:
                o = o * scale
            heads_ref[hd, rs, :] = o.astype(BF16)

    @pl.when(j == G_Q)
    def _():
        for rs, r in row_blocks(normalize=True):
            store_rotary(rs, r, None)

    @pl.when(j == G_K)
    def _():
        for rs, r in row_blocks():
            store_rotary(rs, r, HEAD_DIM ** -0.5)

    @pl.when(j == G_V)
    def _():
        for rs, r in row_blocks():
            store_heads(rs, r)

    @pl.when(j == G_GRET)
    def _():
        for rs, r in row_blocks():
            store_heads(rs, r * jax.nn.sigmoid(r))

    @pl.when(jnp.logical_or(j == G_CB, j == G_CC))
    def _():
        for rs, r in row_blocks():
            stash_ref[j - G_CB, rs, :] = r.astype(BF16)

    @pl.when(j == G_CX)
    def _():
        for rs, r in row_blocks():
            rows_ref[0, rs, :] = (stash_ref[S_CC, rs, :].astype(F32) * r).astype(BF16)

    @pl.when(j == G_GCONV)
    def _():
        for rs, r in row_blocks():
            rows_ref[0, rs, :] = (stash_ref[S_CB, rs, :].astype(F32) * (r * jax.nn.sigmoid(r))
                                  ).astype(BF16)

    @pl.when(j >= G_GATE0)
    def _():
        for rs, r in row_blocks():
            rows_ref[0, rs, :] = jax.nn.sigmoid(r).astype(BF16)


def _project(x2, gain, w_bf, row_tab, tile_tab, seq):
    tokens = x2.shape[0]
    tm = PROJ_ROWS
    seq_tiles = seq // tm
    grid = (tokens // tm, N_COL_TILES)
    return pl.pallas_call(
        _proj_kernel,
        grid=grid,
        in_specs=[
            pl.BlockSpec(memory_space=pl.ANY),
            pl.BlockSpec((1, D_MODEL), lambda i, j: (0, 0)),
            pl.BlockSpec((D_MODEL, WIDTH), lambda i, j: (0, j)),
            pl.BlockSpec((3, tm, HEAD_DIM), lambda i, j: (0, 0, 0)),
            pl.BlockSpec((1, 8, HEAD_DIM), lambda i, j: (i % seq_tiles, 0, 0)),
        ],
        out_specs=[
            pl.BlockSpec((HEADS, tm, HEAD_DIM),
                         lambda i, j: (jnp.minimum(j, N_HEAD_GROUPS - 1), i, 0)),
            pl.BlockSpec((1, tm, WIDTH),
                         lambda i, j: (jnp.maximum(j - G_CX, 0), i, 0)),
        ],
        out_shape=[
            jax.ShapeDtypeStruct((N_HEAD_GROUPS * HEADS, tokens, HEAD_DIM), BF16),
            jax.ShapeDtypeStruct((N_ROW_GROUPS, tokens, WIDTH), BF16),
        ],
        scratch_shapes=[
            pltpu.VMEM((tm, D_MODEL), F32),
            pltpu.SemaphoreType.DMA(()),
            pltpu.VMEM((tm, D_MODEL), BF16),
            pltpu.VMEM((2, tm, WIDTH), BF16),
        ],
        compiler_params=pltpu.CompilerParams(
            dimension_semantics=("arbitrary", "arbitrary"),
            vmem_limit_bytes=VMEM_LIMIT),
        name="proj",
    )(x2, gain, w_bf, row_tab, tile_tab)


def _log_sigmoid(z):
    return -(jnp.maximum(-z, 0.0) + jnp.log1p(jnp.exp(-jnp.abs(z))))


def _ret_kernel(logit_ref, q_ref, k_ref, v_ref, g_ref, gain_ref, o_ref, sf_ref, sb_ref, p_ref, *,
                n_chunks):
    lf = _log_sigmoid(logit_ref[0, 0])[0:1, :]
    lb = _log_sigmoid(logit_ref[0, 1])[0:1, :]
    row = lax.broadcasted_iota(jnp.int32, (CHUNK, CHUNK), 0).astype(F32)
    col = lax.broadcasted_iota(jnp.int32, (CHUNK, CHUNK), 1).astype(F32)
    dist = row - col
    decay = jnp.exp(jnp.where(dist >= 0, dist * lf, -dist * lb))
    wq_f = jnp.exp(row * lf).astype(BF16)
    wq_b = jnp.exp((CHUNK - 1 - row) * lb).astype(BF16)
    wk_f = jnp.exp((CHUNK - row) * lf).astype(BF16)
    wk_b = jnp.exp((row + 1) * lb).astype(BF16)
    chunk_f = jnp.exp(CHUNK * lf)
    chunk_b = jnp.exp(CHUNK * lb)
    gain = gain_ref[0]

    def rows_of(c):
        return pl.ds(pl.multiple_of(c * CHUNK, CHUNK), CHUNK)

    def kv_summary(k_weighted, v):
        return lax.dot_general(k_weighted, v, (((0,), (0,)), ((), ())),
                               preferred_element_type=F32)

    def state_body(i, carry):
        state_f, state_b = carry
        cf = i
        cb = n_chunks - 1 - i
        sf_ref[cf] = state_f.astype(BF16)
        sb_ref[cb] = state_b.astype(BF16)
        rows_f = rows_of(cf)
        rows_b = rows_of(cb)
        k_f = k_ref[0, rows_f, :]
        scores = lax.dot_general(q_ref[0, rows_f, :], k_f, (((1,), (1,)), ((), ())),
                                 preferred_element_type=F32)
        p_ref[rows_f, :] = (scores * decay).astype(BF16)
        return (chunk_f * state_f + kv_summary(k_f * wk_f, v_ref[0, rows_f, :]),
                chunk_b * state_b + kv_summary(k_ref[0, rows_b, :] * wk_b, v_ref[0, rows_b, :]))

    zero = jnp.zeros((CHUNK, HEAD_DIM), F32)
    lax.fori_loop(0, n_chunks, state_body, (zero, zero), unroll=STATE_UNROLL)

    def out_body(c, carry):
        rows = rows_of(c)
        q = q_ref[0, rows, :]
        lhs = jnp.concatenate([p_ref[rows, :], q * wq_f, q * wq_b], axis=1)
        rhs = jnp.concatenate([v_ref[0, rows, :], sf_ref[c], sb_ref[c]], axis=0)
        o = jnp.dot(lhs, rhs, preferred_element_type=F32)
        ms = jnp.mean(o * o, axis=-1, keepdims=True)
        o = o * lax.rsqrt(ms + EPS) * gain * g_ref[0, rows, :].astype(F32)
        o_ref[0, rows, :] = o.astype(BF16)
        return carry

    lax.fori_loop(0, n_chunks, out_body, 0, unroll=OUT_UNROLL)


def _retention(heads, logits, gn_gain, batch, seq):
    tokens = batch * seq
    n_chunks = seq // CHUNK

    def head_spec(group):
        return pl.BlockSpec((1, seq, HEAD_DIM),
                            lambda p: (group * HEADS + p % HEADS, p // HEADS, 0))

    return pl.pallas_call(
        functools.partial(_ret_kernel, n_chunks=n_chunks),
        grid=(batch * HEADS,),
        in_specs=[
            pl.BlockSpec((1, 2, 8, HEAD_DIM), lambda p: (p % HEADS, 0, 0, 0)),
            head_spec(G_Q), head_spec(G_K), head_spec(G_V), head_spec(G_GRET),
            pl.BlockSpec((1, 1, HEAD_DIM), lambda p: (p % HEADS, 0, 0)),
        ],
        out_specs=pl.BlockSpec((1, seq, HEAD_DIM), lambda p: (p % HEADS, p // HEADS, 0)),
        out_shape=jax.ShapeDtypeStruct((HEADS, tokens, HEAD_DIM), BF16),
        scratch_shapes=[pltpu.VMEM((n_chunks, CHUNK, HEAD_DIM), BF16),
                        pltpu.VMEM((n_chunks, CHUNK, HEAD_DIM), BF16),
                        pltpu.VMEM((seq, CHUNK), BF16)],
        compiler_params=pltpu.CompilerParams(
            dimension_semantics=("arbitrary",),
            vmem_limit_bytes=VMEM_LIMIT),
        name="retention",
    )(logits, heads, heads, heads, heads, gn_gain)


def _merge_kernel(rows_ref, uprev_ref, unext_ref, ret_ref, x_ref, convw_ref, wb_ref, wo_ref,
                  fgain_ref, out_ref, *, seq_tiles):
    tm = x_ref.shape[0]
    i = pl.program_id(0)
    pos = i % seq_tiles
    u = rows_ref[R_U].astype(F32)
    prev_row = jnp.where(pos == 0, 0.0, uprev_ref[0, HALO_ROWS - 1:HALO_ROWS, :].astype(F32))
    next_row = jnp.where(pos == seq_tiles - 1, 0.0, unext_ref[0, 0:1, :].astype(F32))
    ridx = lax.broadcasted_iota(jnp.int32, (tm, WIDTH), 0)
    u_prev = jnp.where(ridx == 0, prev_row, pltpu.roll(u, 1, axis=0))
    u_next = jnp.where(ridx == tm - 1, next_row, pltpu.roll(u, tm - 1, axis=0))
    w = convw_ref[...]
    conv = w[0:1, :] * u_prev + w[1:2, :] * u + w[2:3, :] * u_next
    branch_conv = (rows_ref[R_CBG].astype(F32) * conv).astype(BF16)

    def up_project(rs):
        branch_ret = jnp.concatenate([ret_ref[hd, rs, :] for hd in range(HEADS)], axis=1)
        ups = []
        for half in range(D_MODEL // WIDTH):
            cols = slice(half * WIDTH, (half + 1) * WIDTH)
            ups.append((jnp.dot(branch_ret, wb_ref[0, :, cols], preferred_element_type=F32),
                        jnp.dot(branch_conv[rs, :], wb_ref[1, :, cols],
                                preferred_element_type=F32)))
        return ups

    def gated_merge(rs, ups):
        halves = []
        for half, (up_ret, up_conv) in enumerate(ups):
            gate_ret = rows_ref[R_GATE0 + half, rs, :].astype(F32)
            gate_conv = rows_ref[R_GATE0 + D_MODEL // WIDTH + half, rs, :].astype(F32)
            halves.append((gate_ret * up_ret + gate_conv * up_conv).astype(BF16))
        return jnp.concatenate(halves, axis=1)

    def out_project(rs, merged):
        return x_ref[rs, :] + jnp.dot(merged, wo_ref[...], preferred_element_type=F32)

    def normalise(rs, y):
        ms = jnp.mean(y * y, axis=-1, keepdims=True)
        out_ref[rs, :] = y * lax.rsqrt(ms + EPS) * fgain_ref[...]

    blocks = [slice(r0, r0 + MERGE_SUB_ROWS) for r0 in range(0, tm, MERGE_SUB_ROWS)]
    ups = [up_project(rs) for rs in blocks]
    outs = []
    for rs, up in zip(blocks, ups):
        outs.append(out_project(rs, gated_merge(rs, up)))
    for rs, y in zip(blocks, outs):
        normalise(rs, y)


def _merge(rows, ret, x2, conv_w, wb_bf, wo_bf, fgain, seq):
    tokens = x2.shape[0]
    tm = MERGE_ROWS
    seq_tiles = seq // tm
    halo_per_tile = tm // HALO_ROWS
    n_halo = tokens // HALO_ROWS
    resident = pl.Buffered(1)
    return pl.pallas_call(
        functools.partial(_merge_kernel, seq_tiles=seq_tiles),
        grid=(tokens // tm,),
        in_specs=[
            pl.BlockSpec((N_ROW_GROUPS, tm, WIDTH), lambda i: (0, i, 0)),
            pl.BlockSpec((1, HALO_ROWS, WIDTH),
                         lambda i: (R_U, jnp.maximum(i * halo_per_tile - 1, 0), 0)),
            pl.BlockSpec((1, HALO_ROWS, WIDTH),
                         lambda i: (R_U, jnp.minimum((i + 1) * halo_per_tile, n_halo - 1), 0)),
            pl.BlockSpec((HEADS, tm, HEAD_DIM), lambda i: (0, i, 0)),
            pl.BlockSpec((tm, D_MODEL), lambda i: (i, 0)),
            pl.BlockSpec((CONV_K, WIDTH), lambda i: (0, 0)),
            pl.BlockSpec((2, WIDTH, D_MODEL), lambda i: (0, 0, 0), pipeline_mode=resident),
            pl.BlockSpec((D_MODEL, D_MODEL), lambda i: (0, 0), pipeline_mode=resident),
            pl.BlockSpec((1, D_MODEL), lambda i: (0, 0)),
        ],
        out_specs=pl.BlockSpec((tm, D_MODEL), lambda i: (i, 0)),
        out_shape=jax.ShapeDtypeStruct((tokens, D_MODEL), F32),
        compiler_params=pltpu.CompilerParams(
            dimension_semantics=("arbitrary",),
            vmem_limit_bytes=VMEM_LIMIT),
        name="merge",
    )(rows, rows, rows, ret, x2, conv_w, wb_bf, wo_bf, fgain)


def _rotary_tables(seq, tile_rows):
    inv_freq = ROPE_BASE ** (-jnp.arange(0, HEAD_DIM, 2, dtype=F32) / HEAD_DIM)
    inv_freq = jnp.concatenate([inv_freq, inv_freq])
    sign = jnp.where(jnp.arange(HEAD_DIM) < HEAD_DIM // 2, -1.0, 1.0).astype(F32)

    def table(pos):
        ang = pos.astype(F32)[:, None] * inv_freq[None, :]
        return jnp.stack([jnp.cos(ang), jnp.sin(ang), sign * jnp.sin(ang)])

    row_tab = table(jnp.arange(tile_rows))
    tile_tab = jnp.transpose(table(jnp.arange(seq // tile_rows) * tile_rows), (1, 0, 2))
    return row_tab, jnp.pad(tile_tab, ((0, 0), (0, 8 - 3), (0, 0)))


def kernel(x, norm_gain, w_in, decay_logit_fwd, decay_logit_bwd, ret_gn_gain, conv_w, w_branch,
           w_out, final_gain):
    batch, seq, d_model = x.shape
    assert d_model == D_MODEL and norm_gain.shape[0] == 1
    assert w_in.shape == (1, D_MODEL, N_COL_TILES * WIDTH)
    assert seq % PROJ_ROWS == 0 and seq % MERGE_ROWS == 0 and seq % CHUNK == 0
    tokens = batch * seq
    x2 = x.reshape(tokens, D_MODEL)
    row_tab, tile_tab = _rotary_tables(seq, PROJ_ROWS)
    heads, rows = _project(x2, norm_gain[0][None, :], w_in[0].astype(BF16), row_tab, tile_tab, seq)

    logits = jnp.stack([decay_logit_fwd[0], decay_logit_bwd[0]], axis=1).astype(F32)
    logits = jnp.broadcast_to(logits[:, :, None, None], (HEADS, 2, 8, HEAD_DIM))
    gn_gain = ret_gn_gain[0].astype(F32).reshape(HEADS, 1, HEAD_DIM)
    ret = _retention(heads, logits, gn_gain, batch, seq)

    out = _merge(rows, ret, x2, conv_w[0], w_branch[0].astype(BF16), w_out[0].astype(BF16),
                 final_gain[None, :], seq)
    return out.reshape(batch, seq, D_MODEL)
```

```python
import functools

import jax
import jax.numpy as jnp
from jax import lax
from jax.experimental import pallas as pl
from jax.experimental.pallas import tpu as pltpu

F32 = jnp.float32
BF16 = jnp.bfloat16

D_MODEL = 2048
HEADS = 8
HEAD_DIM = 128
WIDTH = HEADS * HEAD_DIM
CHUNK = 128
CONV_K = 3
ROPE_BASE = 10000.0
EPS = 1e-6
G_Q, G_K, G_V, G_GRET, G_CB, G_CC, G_CX, G_GCONV, G_GATE0 = range(9)
N_COL_TILES = 12
N_HEAD_GROUPS = 4
R_U, R_CBG, R_GATE0 = 0, 1, 2
N_ROW_GROUPS = 6
S_CB, S_CC = 0, 1

PROJ_ROWS = 1024
PROJ_SUB_ROWS = 256
MERGE_ROWS = 512
MERGE_SUB_ROWS = 256
STATE_UNROLL = 64
OUT_UNROLL = 64
HALO_ROWS = 16
VMEM_LIMIT = 58 * 1024 * 1024


def _proj_kernel(x_hbm, gain_ref, w_ref, rowtab_ref, tiletab_ref, heads_ref, rows_ref, x_ref,
                 x_sem, h_ref, stash_ref):
    i = pl.program_id(0)
    j = pl.program_id(1)
    tm = h_ref.shape[0]

    def x_copy(tile):
        return pltpu.make_async_copy(x_hbm.at[pl.ds(tile * tm, tm), :], x_ref, x_sem)

    @pl.when(jnp.logical_and(i == 0, j == 0))
    def _():
        x_copy(0).start()

    @pl.when(jnp.logical_and(j == 1, i + 1 < pl.num_programs(0)))
    def _():
        x_copy(i + 1).start()

    @pl.when(j == 0)
    def _():
        x_copy(i).wait()

    def row_blocks(normalize=False):
        for r0 in range(0, h_ref.shape[0], PROJ_SUB_ROWS):
            rs = slice(r0, r0 + PROJ_SUB_ROWS)
            if normalize:
                x = x_ref[rs, :]
                ms = jnp.mean(x * x, axis=-1, keepdims=True)
                h = (x * lax.rsqrt(ms + EPS) * gain_ref[...]).astype(BF16)
                h_ref[rs, :] = h
            else:
                h = h_ref[rs, :]
            yield rs, jnp.dot(h, w_ref[0], preferred_element_type=F32)

    def store_heads(rs, r):
        for hd in range(HEADS):
            heads_ref[hd, rs, :] = r[:, hd * HEAD_DIM:(hd + 1) * HEAD_DIM].astype(BF16)

    def store_rotary(rs, r, scale):
        tile_cos, tile_sin, tile_ssin = (tiletab_ref[0, t:t + 1, :] for t in range(3))
        cos = tile_cos * rowtab_ref[0, rs, :] - tile_sin * rowtab_ref[1, rs, :]
        sin = tile_ssin * rowtab_ref[0, rs, :] + tile_cos * rowtab_ref[2, rs, :]
        for hd in range(HEADS):
            rh = r[:, hd * HEAD_DIM:(hd + 1) * HEAD_DIM]
            o = rh * cos + pltpu.roll(rh, HEAD_DIM // 2, axis=1) * sin
            if scale is not None:
                o = o * scale
            heads_ref[hd, rs, :] = o.astype(BF16)

    @pl.when(j == G_Q)
    def _():
        for rs, r in row_blocks(normalize=True):
            store_rotary(rs, r, None)

    @pl.when(j == G_K)
    def _():
        for rs, r in row_blocks():
            store_rotary(rs, r, HEAD_DIM ** -0.5)

    @pl.when(j == G_V)
    def _():
        for rs, r in row_blocks():
            store_heads(rs, r)

    @pl.when(j == G_GRET)
    def _():
        for rs, r in row_blocks():
            store_heads(rs, r * jax.nn.sigmoid(r))

    @pl.when(jnp.logical_or(j == G_CB, j == G_CC))
    def _():
        for rs, r in row_blocks():
            stash_ref[j - G_CB, rs, :] = r.astype(BF16)

    @pl.when(j == G_CX)
    def _():
        for rs, r in row_blocks():
            rows_ref[0, rs, :] = (stash_ref[S_CC, rs, :].astype(F32) * r).astype(BF16)

    @pl.when(j == G_GCONV)
    def _():
        for rs, r in row_blocks():
            rows_ref[0, rs, :] = (stash_ref[S_CB, rs, :].astype(F32) * (r * jax.nn.sigmoid(r))
                                  ).astype(BF16)

    @pl.when(j >= G_GATE0)
    def _():
        for rs, r in row_blocks():
            rows_ref[0, rs, :] = jax.nn.sigmoid(r).astype(BF16)


def _project(x2, gain, w_bf, row_tab, tile_tab, seq):
    tokens = x2.shape[0]
    tm = PROJ_ROWS
    seq_tiles = seq // tm
    grid = (tokens // tm, N_COL_TILES)
    return pl.pallas_call(
        _proj_kernel,
        grid=grid,
        in_specs=[
            pl.BlockSpec(memory_space=pl.ANY),
            pl.BlockSpec((1, D_MODEL), lambda i, j: (0, 0)),
            pl.BlockSpec((1, D_MODEL, WIDTH), lambda i, j: (j, 0, 0)),
            pl.BlockSpec((3, tm, HEAD_DIM), lambda i, j: (0, 0, 0)),
            pl.BlockSpec((1, 8, HEAD_DIM), lambda i, j: (i % seq_tiles, 0, 0)),
        ],
        out_specs=[
            pl.BlockSpec((HEADS, tm, HEAD_DIM),
                         lambda i, j: (jnp.minimum(j, N_HEAD_GROUPS - 1), i, 0)),
            pl.BlockSpec((1, tm, WIDTH),
                         lambda i, j: (jnp.maximum(j - G_CX, 0), i, 0)),
        ],
        out_shape=[
            jax.ShapeDtypeStruct((N_HEAD_GROUPS * HEADS, tokens, HEAD_DIM), BF16),
            jax.ShapeDtypeStruct((N_ROW_GROUPS, tokens, WIDTH), BF16),
        ],
        scratch_shapes=[
            pltpu.VMEM((tm, D_MODEL), F32),
            pltpu.SemaphoreType.DMA(()),
            pltpu.VMEM((tm, D_MODEL), BF16),
            pltpu.VMEM((2, tm, WIDTH), BF16),
        ],
        compiler_params=pltpu.CompilerParams(
            dimension_semantics=("arbitrary", "arbitrary"),
            vmem_limit_bytes=VMEM_LIMIT),
        name="proj",
    )(x2, gain, w_bf, row_tab, tile_tab)


def _log_sigmoid(z):
    return -(jnp.maximum(-z, 0.0) + jnp.log1p(jnp.exp(-jnp.abs(z))))


def _ret_kernel(logit_ref, q_ref, k_ref, v_ref, g_ref, gain_ref, o_ref, sf_ref, sb_ref, p_ref, *,
                n_chunks):
    lf = _log_sigmoid(logit_ref[0, 0])[0:1, :]
    lb = _log_sigmoid(logit_ref[0, 1])[0:1, :]
    row = lax.broadcasted_iota(jnp.int32, (CHUNK, CHUNK), 0).astype(F32)
    col = lax.broadcasted_iota(jnp.int32, (CHUNK, CHUNK), 1).astype(F32)
    dist = row - col
    decay = jnp.exp(jnp.where(dist >= 0, dist * lf, -dist * lb))
    wq_f = jnp.exp(row * lf).astype(BF16)
    wq_b = jnp.exp((CHUNK - 1 - row) * lb).astype(BF16)
    wk_f = jnp.exp((CHUNK - row) * lf).astype(BF16)
    wk_b = jnp.exp((row + 1) * lb).astype(BF16)
    chunk_f = jnp.exp(CHUNK * lf)
    chunk_b = jnp.exp(CHUNK * lb)
    gain = gain_ref[0]

    def rows_of(c):
        return pl.ds(pl.multiple_of(c * CHUNK, CHUNK), CHUNK)

    def kv_summary(k_weighted, v):
        return lax.dot_general(k_weighted, v, (((0,), (0,)), ((), ())),
                               preferred_element_type=F32)

    def state_body(i, carry):
        state_f, state_b = carry
        cf = i
        cb = n_chunks - 1 - i
        sf_ref[cf] = state_f.astype(BF16)
        sb_ref[cb] = state_b.astype(BF16)
        rows_f = rows_of(cf)
        rows_b = rows_of(cb)
        k_f = k_ref[0, rows_f, :]
        scores = lax.dot_general(q_ref[0, rows_f, :], k_f, (((1,), (1,)), ((), ())),
                                 preferred_element_type=F32)
        p_ref[rows_f, :] = (scores * decay).astype(BF16)
        return (chunk_f * state_f + kv_summary(k_f * wk_f, v_ref[0, rows_f, :]),
                chunk_b * state_b + kv_summary(k_ref[0, rows_b, :] * wk_b, v_ref[0, rows_b, :]))

    zero = jnp.zeros((CHUNK, HEAD_DIM), F32)
    lax.fori_loop(0, n_chunks, state_body, (zero, zero), unroll=STATE_UNROLL)

    def out_body(c, carry):
        rows = rows_of(c)
        q = q_ref[0, rows, :]
        lhs = jnp.concatenate([p_ref[rows, :], q * wq_f, q * wq_b], axis=1)
        rhs = jnp.concatenate([v_ref[0, rows, :], sf_ref[c], sb_ref[c]], axis=0)
        o = jnp.dot(lhs, rhs, preferred_element_type=F32)
        ms = jnp.mean(o * o, axis=-1, keepdims=True)
        o = o * lax.rsqrt(ms + EPS) * gain * g_ref[0, rows, :].astype(F32)
        o_ref[0, rows, :] = o.astype(BF16)
        return carry

    lax.fori_loop(0, n_chunks, out_body, 0, unroll=OUT_UNROLL)


def _retention(heads, logits, gn_gain, batch, seq):
    tokens = batch * seq
    n_chunks = seq // CHUNK

    def head_spec(group):
        return pl.BlockSpec((1, seq, HEAD_DIM),
                            lambda p: (group * HEADS + p % HEADS, p // HEADS, 0))

    return pl.pallas_call(
        functools.partial(_ret_kernel, n_chunks=n_chunks),
        grid=(batch * HEADS,),
        in_specs=[
            pl.BlockSpec((1, 2, 8, HEAD_DIM), lambda p: (p % HEADS, 0, 0, 0)),
            head_spec(G_Q), head_spec(G_K), head_spec(G_V), head_spec(G_GRET),
            pl.BlockSpec((1, 1, HEAD_DIM), lambda p: (p % HEADS, 0, 0)),
        ],
        out_specs=pl.BlockSpec((1, seq, HEAD_DIM), lambda p: (p % HEADS, p // HEADS, 0)),
        out_shape=jax.ShapeDtypeStruct((HEADS, tokens, HEAD_DIM), BF16),
        scratch_shapes=[pltpu.VMEM((n_chunks, CHUNK, HEAD_DIM), BF16),
                        pltpu.VMEM((n_chunks, CHUNK, HEAD_DIM), BF16),
                        pltpu.VMEM((seq, CHUNK), BF16)],
        compiler_params=pltpu.CompilerParams(
            dimension_semantics=("arbitrary",),
            vmem_limit_bytes=VMEM_LIMIT),
        name="retention",
    )(logits, heads, heads, heads, heads, gn_gain)


def _merge_kernel(rows_ref, uprev_ref, unext_ref, ret_ref, x_ref, convw_ref, wb_ref, wo_ref,
                  fgain_ref, out_ref, *, seq_tiles):
    tm = x_ref.shape[0]
    i = pl.program_id(0)
    pos = i % seq_tiles
    u = rows_ref[R_U].astype(F32)
    prev_row = jnp.where(pos == 0, 0.0, uprev_ref[0, HALO_ROWS - 1:HALO_ROWS, :].astype(F32))
    next_row = jnp.where(pos == seq_tiles - 1, 0.0, unext_ref[0, 0:1, :].astype(F32))
    ridx = lax.broadcasted_iota(jnp.int32, (tm, WIDTH), 0)
    u_prev = jnp.where(ridx == 0, prev_row, pltpu.roll(u, 1, axis=0))
    u_next = jnp.where(ridx == tm - 1, next_row, pltpu.roll(u, tm - 1, axis=0))
    w = convw_ref[...]
    conv = w[0:1, :] * u_prev + w[1:2, :] * u + w[2:3, :] * u_next
    branch_conv = (rows_ref[R_CBG].astype(F32) * conv).astype(BF16)

    def up_project(rs):
        branch_ret = jnp.concatenate([ret_ref[hd, rs, :] for hd in range(HEADS)], axis=1)
        ups = []
        for half in range(D_MODEL // WIDTH):
            cols = slice(half * WIDTH, (half + 1) * WIDTH)
            ups.append((jnp.dot(branch_ret, wb_ref[0, :, cols], preferred_element_type=F32),
                        jnp.dot(branch_conv[rs, :], wb_ref[1, :, cols],
                                preferred_element_type=F32)))
        return ups

    def gated_merge(rs, ups):
        halves = []
        for half, (up_ret, up_conv) in enumerate(ups):
            gate_ret = rows_ref[R_GATE0 + half, rs, :].astype(F32)
            gate_conv = rows_ref[R_GATE0 + D_MODEL // WIDTH + half, rs, :].astype(F32)
            halves.append((gate_ret * up_ret + gate_conv * up_conv).astype(BF16))
        return jnp.concatenate(halves, axis=1)

    def out_project(rs, merged):
        return x_ref[rs, :] + jnp.dot(merged, wo_ref[...], preferred_element_type=F32)

    def normalise(rs, y):
        ms = jnp.mean(y * y, axis=-1, keepdims=True)
        out_ref[rs, :] = y * lax.rsqrt(ms + EPS) * fgain_ref[...]

    blocks = [slice(r0, r0 + MERGE_SUB_ROWS) for r0 in range(0, tm, MERGE_SUB_ROWS)]
    ups = [up_project(rs) for rs in blocks]
    outs = []
    for rs, up in zip(blocks, ups):
        outs.append(out_project(rs, gated_merge(rs, up)))
    for rs, y in zip(blocks, outs):
        normalise(rs, y)


def _merge(rows, ret, x2, conv_w, wb_bf, wo_bf, fgain, seq):
    tokens = x2.shape[0]
    tm = MERGE_ROWS
    seq_tiles = seq // tm
    halo_per_tile = tm // HALO_ROWS
    n_halo = tokens // HALO_ROWS
    resident = pl.Buffered(1)
    return pl.pallas_call(
        functools.partial(_merge_kernel, seq_tiles=seq_tiles),
        grid=(tokens // tm,),
        in_specs=[
            pl.BlockSpec((N_ROW_GROUPS, tm, WIDTH), lambda i: (0, i, 0)),
            pl.BlockSpec((1, HALO_ROWS, WIDTH),
                         lambda i: (R_U, jnp.maximum(i * halo_per_tile - 1, 0), 0)),
            pl.BlockSpec((1, HALO_ROWS, WIDTH),
                         lambda i: (R_U, jnp.minimum((i + 1) * halo_per_tile, n_halo - 1), 0)),
            pl.BlockSpec((HEADS, tm, HEAD_DIM), lambda i: (0, i, 0)),
            pl.BlockSpec((tm, D_MODEL), lambda i: (i, 0)),
            pl.BlockSpec((CONV_K, WIDTH), lambda i: (0, 0)),
            pl.BlockSpec((2, WIDTH, D_MODEL), lambda i: (0, 0, 0), pipeline_mode=resident),
            pl.BlockSpec((D_MODEL, D_MODEL), lambda i: (0, 0), pipeline_mode=resident),
            pl.BlockSpec((1, D_MODEL), lambda i: (0, 0)),
        ],
        out_specs=pl.BlockSpec((tm, D_MODEL), lambda i: (i, 0)),
        out_shape=jax.ShapeDtypeStruct((tokens, D_MODEL), F32),
        compiler_params=pltpu.CompilerParams(
            dimension_semantics=("arbitrary",),
            vmem_limit_bytes=VMEM_LIMIT),
        name="merge",
    )(rows, rows, rows, ret, x2, conv_w, wb_bf, wo_bf, fgain)


def _rotary_tables(seq, tile_rows):
    inv_freq = ROPE_BASE ** (-jnp.arange(0, HEAD_DIM, 2, dtype=F32) / HEAD_DIM)
    inv_freq = jnp.concatenate([inv_freq, inv_freq])
    sign = jnp.where(jnp.arange(HEAD_DIM) < HEAD_DIM // 2, -1.0, 1.0).astype(F32)

    def table(pos):
        ang = pos.astype(F32)[:, None] * inv_freq[None, :]
        return jnp.stack([jnp.cos(ang), jnp.sin(ang), sign * jnp.sin(ang)])

    row_tab = table(jnp.arange(tile_rows))
    tile_tab = jnp.transpose(table(jnp.arange(seq // tile_rows) * tile_rows), (1, 0, 2))
    return row_tab, jnp.pad(tile_tab, ((0, 0), (0, 8 - 3), (0, 0)))


def kernel(x, norm_gain, w_in, decay_logit_fwd, decay_logit_bwd, ret_gn_gain, conv_w, w_branch,
           w_out, final_gain):
    batch, seq, d_model = x.shape
    assert d_model == D_MODEL and norm_gain.shape[0] == 1
    assert w_in.shape == (1, D_MODEL, N_COL_TILES * WIDTH)
    assert seq % PROJ_ROWS == 0 and seq % MERGE_ROWS == 0 and seq % CHUNK == 0
    tokens = batch * seq
    x2 = x.reshape(tokens, D_MODEL)
    row_tab, tile_tab = _rotary_tables(seq, PROJ_ROWS)
    w_tiles = jnp.transpose(w_in[0].astype(BF16).reshape(D_MODEL, N_COL_TILES, WIDTH), (1, 0, 2))
    heads, rows = _project(x2, norm_gain[0][None, :], w_tiles, row_tab, tile_tab, seq)

    logits = jnp.stack([decay_logit_fwd[0], decay_logit_bwd[0]], axis=1).astype(F32)
    logits = jnp.broadcast_to(logits[:, :, None, None], (HEADS, 2, 8, HEAD_DIM))
    gn_gain = ret_gn_gain[0].astype(F32).reshape(HEADS, 1, HEAD_DIM)
    ret = _retention(heads, logits, gn_gain, batch, seq)

    out = _merge(rows, ret, x2, conv_w[0], w_branch[0].astype(BF16), w_out[0].astype(BF16),
                 final_gain[None, :], seq)
    return out.reshape(batch, seq, D_MODEL)
```

```python
import functools

import jax
import jax.numpy as jnp
from jax import lax
from jax.experimental import pallas as pl
from jax.experimental.pallas import tpu as pltpu

F32 = jnp.float32
BF16 = jnp.bfloat16

D_MODEL = 2048
HEADS = 8
HEAD_DIM = 128
WIDTH = HEADS * HEAD_DIM
CHUNK = 128
CONV_K = 3
ROPE_BASE = 10000.0
EPS = 1e-6
G_Q, G_K, G_V, G_GRET, G_CB, G_CC, G_CX, G_GCONV, G_GATE0 = range(9)
N_COL_TILES = 12
N_HEAD_GROUPS = 4
R_U, R_CBG, R_GATE0 = 0, 1, 2
N_ROW_GROUPS = 6
S_CB, S_CC = 0, 1

PROJ_ROWS = 1024
PROJ_SUB_ROWS = 256
MERGE_ROWS = 512
MERGE_SUB_ROWS = 256
STATE_UNROLL = 64
OUT_UNROLL = 64
HALO_ROWS = 16
VMEM_LIMIT = 58 * 1024 * 1024


def _proj_kernel(x_hbm, gain_ref, gngain_ref, w_ref, rowtab_ref, tiletab_ref, heads_ref, rows_ref,
                 x_ref, x_sem, h_ref, stash_ref):
    i = pl.program_id(0)
    j = pl.program_id(1)
    tm = h_ref.shape[0]

    def x_copy(tile):
        return pltpu.make_async_copy(x_hbm.at[pl.ds(tile * tm, tm), :], x_ref, x_sem)

    @pl.when(jnp.logical_and(i == 0, j == 0))
    def _():
        x_copy(0).start()

    @pl.when(jnp.logical_and(j == 1, i + 1 < pl.num_programs(0)))
    def _():
        x_copy(i + 1).start()

    @pl.when(j == 0)
    def _():
        x_copy(i).wait()

    def row_blocks(normalize=False):
        for r0 in range(0, h_ref.shape[0], PROJ_SUB_ROWS):
            rs = slice(r0, r0 + PROJ_SUB_ROWS)
            if normalize:
                x = x_ref[rs, :]
                ms = jnp.mean(x * x, axis=-1, keepdims=True)
                h = (x * lax.rsqrt(ms + EPS) * gain_ref[...]).astype(BF16)
                h_ref[rs, :] = h
            else:
                h = h_ref[rs, :]
            yield rs, jnp.dot(h, w_ref[...], preferred_element_type=F32)

    def store_heads(rs, r):
        for hd in range(HEADS):
            heads_ref[hd, rs, :] = r[:, hd * HEAD_DIM:(hd + 1) * HEAD_DIM].astype(BF16)

    def store_rotary(rs, r, scale):
        tile_cos, tile_sin, tile_ssin = (tiletab_ref[0, t:t + 1, :] for t in range(3))
        cos = tile_cos * rowtab_ref[0, rs, :] - tile_sin * rowtab_ref[1, rs, :]
        sin = tile_ssin * rowtab_ref[0, rs, :] + tile_cos * rowtab_ref[2, rs, :]
        for hd in range(HEADS):
            rh = r[:, hd * HEAD_DIM:(hd + 1) * HEAD_DIM]
            o = rh * cos + pltpu.roll(rh, HEAD_DIM // 2, axis=1) * sin
            if scale is not None:
                o = o * scale
            heads_ref[hd, rs, :] = o.astype(BF16)

    @pl.when(j == G_Q)
    def _():
        for rs, r in row_blocks(normalize=True):
            store_rotary(rs, r, None)

    @pl.when(j == G_K)
    def _():
        for rs, r in row_blocks():
            store_rotary(rs, r, HEAD_DIM ** -0.5)

    @pl.when(j == G_V)
    def _():
        for rs, r in row_blocks():
            store_heads(rs, r)

    @pl.when(j == G_GRET)
    def _():
        for rs, r in row_blocks():
            store_heads(rs, r * jax.nn.sigmoid(r) * gngain_ref[...])

    @pl.when(jnp.logical_or(j == G_CB, j == G_CC))
    def _():
        for rs, r in row_blocks():
            stash_ref[j - G_CB, rs, :] = r.astype(BF16)

    @pl.when(j == G_CX)
    def _():
        for rs, r in row_blocks():
            rows_ref[0, rs, :] = (stash_ref[S_CC, rs, :].astype(F32) * r).astype(BF16)

    @pl.when(j == G_GCONV)
    def _():
        for rs, r in row_blocks():
            rows_ref[0, rs, :] = (stash_ref[S_CB, rs, :].astype(F32) * (r * jax.nn.sigmoid(r))
                                  ).astype(BF16)

    @pl.when(j >= G_GATE0)
    def _():
        for rs, r in row_blocks():
            rows_ref[0, rs, :] = jax.nn.sigmoid(r).astype(BF16)


def _project(x2, gain, gn_gain, w_bf, row_tab, tile_tab, seq):
    tokens = x2.shape[0]
    tm = PROJ_ROWS
    seq_tiles = seq // tm
    grid = (tokens // tm, N_COL_TILES)
    return pl.pallas_call(
        _proj_kernel,
        grid=grid,
        in_specs=[
            pl.BlockSpec(memory_space=pl.ANY),
            pl.BlockSpec((1, D_MODEL), lambda i, j: (0, 0)),
            pl.BlockSpec((1, WIDTH), lambda i, j: (0, 0)),
            pl.BlockSpec((D_MODEL, WIDTH), lambda i, j: (0, j)),
            pl.BlockSpec((3, tm, HEAD_DIM), lambda i, j: (0, 0, 0)),
            pl.BlockSpec((1, 8, HEAD_DIM), lambda i, j: (i % seq_tiles, 0, 0)),
        ],
        out_specs=[
            pl.BlockSpec((HEADS, tm, HEAD_DIM),
                         lambda i, j: (jnp.minimum(j, N_HEAD_GROUPS - 1), i, 0)),
            pl.BlockSpec((1, tm, WIDTH),
                         lambda i, j: (jnp.maximum(j - G_CX, 0), i, 0)),
        ],
        out_shape=[
            jax.ShapeDtypeStruct((N_HEAD_GROUPS * HEADS, tokens, HEAD_DIM), BF16),
            jax.ShapeDtypeStruct((N_ROW_GROUPS, tokens, WIDTH), BF16),
        ],
        scratch_shapes=[
            pltpu.VMEM((tm, D_MODEL), F32),
            pltpu.SemaphoreType.DMA(()),
            pltpu.VMEM((tm, D_MODEL), BF16),
            pltpu.VMEM((2, tm, WIDTH), BF16),
        ],
        compiler_params=pltpu.CompilerParams(
            dimension_semantics=("arbitrary", "arbitrary"),
            vmem_limit_bytes=VMEM_LIMIT),
        name="proj",
    )(x2, gain, gn_gain, w_bf, row_tab, tile_tab)


def _log_sigmoid(z):
    return -(jnp.maximum(-z, 0.0) + jnp.log1p(jnp.exp(-jnp.abs(z))))


def _ret_kernel(logit_ref, q_ref, k_ref, v_ref, g_ref, o_ref, sf_ref, sb_ref, p_ref, *, n_chunks):
    lf = _log_sigmoid(logit_ref[0, 0])[0:1, :]
    lb = _log_sigmoid(logit_ref[0, 1])[0:1, :]
    row = lax.broadcasted_iota(jnp.int32, (CHUNK, CHUNK), 0).astype(F32)
    col = lax.broadcasted_iota(jnp.int32, (CHUNK, CHUNK), 1).astype(F32)
    dist = row - col
    decay = jnp.exp(jnp.where(dist >= 0, dist * lf, -dist * lb))
    wq_f = jnp.exp(row * lf).astype(BF16)
    wq_b = jnp.exp((CHUNK - 1 - row) * lb).astype(BF16)
    wk_f = jnp.exp((CHUNK - row) * lf).astype(BF16)
    wk_b = jnp.exp((row + 1) * lb).astype(BF16)
    chunk_f = jnp.exp(CHUNK * lf)
    chunk_b = jnp.exp(CHUNK * lb)

    def rows_of(c):
        return pl.ds(pl.multiple_of(c * CHUNK, CHUNK), CHUNK)

    def kv_summary(k_weighted, v):
        return lax.dot_general(k_weighted, v, (((0,), (0,)), ((), ())),
                               preferred_element_type=F32)

    def state_body(i, carry):
        state_f, state_b = carry
        cf = i
        cb = n_chunks - 1 - i
        sf_ref[cf] = state_f.astype(BF16)
        sb_ref[cb] = state_b.astype(BF16)
        rows_f = rows_of(cf)
        rows_b = rows_of(cb)
        k_f = k_ref[0, rows_f, :]
        scores = lax.dot_general(q_ref[0, rows_f, :], k_f, (((1,), (1,)), ((), ())),
                                 preferred_element_type=F32)
        p_ref[rows_f, :] = (scores * decay).astype(BF16)
        return (chunk_f * state_f + kv_summary(k_f * wk_f, v_ref[0, rows_f, :]),
                chunk_b * state_b + kv_summary(k_ref[0, rows_b, :] * wk_b, v_ref[0, rows_b, :]))

    zero = jnp.zeros((CHUNK, HEAD_DIM), F32)
    lax.fori_loop(0, n_chunks, state_body, (zero, zero), unroll=STATE_UNROLL)

    def out_body(c, carry):
        rows = rows_of(c)
        q = q_ref[0, rows, :]
        lhs = jnp.concatenate([p_ref[rows, :], q * wq_f, q * wq_b], axis=1)
        rhs = jnp.concatenate([v_ref[0, rows, :], sf_ref[c], sb_ref[c]], axis=0)
        o = jnp.dot(lhs, rhs, preferred_element_type=F32)
        ms = jnp.mean(o * o, axis=-1, keepdims=True)
        o_ref[0, rows, :] = (o * lax.rsqrt(ms + EPS)).astype(BF16) * g_ref[0, rows, :]
        return carry

    lax.fori_loop(0, n_chunks, out_body, 0, unroll=OUT_UNROLL)


def _retention(heads, logits, batch, seq):
    tokens = batch * seq
    n_chunks = seq // CHUNK

    def head_spec(group):
        return pl.BlockSpec((1, seq, HEAD_DIM),
                            lambda p: (group * HEADS + p % HEADS, p // HEADS, 0))

    return pl.pallas_call(
        functools.partial(_ret_kernel, n_chunks=n_chunks),
        grid=(batch * HEADS,),
        in_specs=[
            pl.BlockSpec((1, 2, 8, HEAD_DIM), lambda p: (p % HEADS, 0, 0, 0)),
            head_spec(G_Q), head_spec(G_K), head_spec(G_V), head_spec(G_GRET),
        ],
        out_specs=pl.BlockSpec((1, seq, HEAD_DIM), lambda p: (p % HEADS, p // HEADS, 0)),
        out_shape=jax.ShapeDtypeStruct((HEADS, tokens, HEAD_DIM), BF16),
        scratch_shapes=[pltpu.VMEM((n_chunks, CHUNK, HEAD_DIM), BF16),
                        pltpu.VMEM((n_chunks, CHUNK, HEAD_DIM), BF16),
                        pltpu.VMEM((seq, CHUNK), BF16)],
        compiler_params=pltpu.CompilerParams(
            dimension_semantics=("arbitrary",),
            vmem_limit_bytes=VMEM_LIMIT),
        name="retention",
    )(logits, heads, heads, heads, heads)


def _merge_kernel(rows_ref, uprev_ref, unext_ref, ret_ref, x_ref, convw_ref, wb_ref, wo_ref,
                  fgain_ref, out_ref, *, seq_tiles):
    tm = x_ref.shape[0]
    i = pl.program_id(0)
    pos = i % seq_tiles
    u = rows_ref[R_U].astype(F32)
    prev_row = jnp.where(pos == 0, 0.0, uprev_ref[0, HALO_ROWS - 1:HALO_ROWS, :].astype(F32))
    next_row = jnp.where(pos == seq_tiles - 1, 0.0, unext_ref[0, 0:1, :].astype(F32))
    ridx = lax.broadcasted_iota(jnp.int32, (tm, WIDTH), 0)
    u_prev = jnp.where(ridx == 0, prev_row, pltpu.roll(u, 1, axis=0))
    u_next = jnp.where(ridx == tm - 1, next_row, pltpu.roll(u, tm - 1, axis=0))
    w = convw_ref[...]
    conv = w[0:1, :] * u_prev + w[1:2, :] * u + w[2:3, :] * u_next
    branch_conv = (rows_ref[R_CBG].astype(F32) * conv).astype(BF16)

    def up_project(rs):
        branch_ret = jnp.concatenate([ret_ref[hd, rs, :] for hd in range(HEADS)], axis=1)
        ups = []
        for half in range(D_MODEL // WIDTH):
            cols = slice(half * WIDTH, (half + 1) * WIDTH)
            ups.append((jnp.dot(branch_ret, wb_ref[0, :, cols], preferred_element_type=F32),
                        jnp.dot(branch_conv[rs, :], wb_ref[1, :, cols],
                                preferred_element_type=F32)))
        return ups

    def gated_merge(rs, ups):
        halves = []
        for half, (up_ret, up_conv) in enumerate(ups):
            gate_ret = rows_ref[R_GATE0 + half, rs, :].astype(F32)
            gate_conv = rows_ref[R_GATE0 + D_MODEL // WIDTH + half, rs, :].astype(F32)
            halves.append((gate_ret * up_ret + gate_conv * up_conv).astype(BF16))
        return jnp.concatenate(halves, axis=1)

    def out_project(rs, merged):
        return x_ref[rs, :] + jnp.dot(merged, wo_ref[...], preferred_element_type=F32)

    def normalise(rs, y):
        ms = jnp.mean(y * y, axis=-1, keepdims=True)
        out_ref[rs, :] = y * lax.rsqrt(ms + EPS) * fgain_ref[...]

    blocks = [slice(r0, r0 + MERGE_SUB_ROWS) for r0 in range(0, tm, MERGE_SUB_ROWS)]
    ups = [up_project(rs) for rs in blocks]
    outs = []
    for rs, up in zip(blocks, ups):
        outs.append(out_project(rs, gated_merge(rs, up)))
    for rs, y in zip(blocks, outs):
        normalise(rs, y)


def _merge(rows, ret, x2, conv_w, wb_bf, wo_bf, fgain, seq):
    tokens = x2.shape[0]
    tm = MERGE_ROWS
    seq_tiles = seq // tm
    halo_per_tile = tm // HALO_ROWS
    n_halo = tokens // HALO_ROWS
    resident = pl.Buffered(1)
    return pl.pallas_call(
        functools.partial(_merge_kernel, seq_tiles=seq_tiles),
        grid=(tokens // tm,),
        in_specs=[
            pl.BlockSpec((N_ROW_GROUPS, tm, WIDTH), lambda i: (0, i, 0)),
            pl.BlockSpec((1, HALO_ROWS, WIDTH),
                         lambda i: (R_U, jnp.maximum(i * halo_per_tile - 1, 0), 0)),
            pl.BlockSpec((1, HALO_ROWS, WIDTH),
                         lambda i: (R_U, jnp.minimum((i + 1) * halo_per_tile, n_halo - 1), 0)),
            pl.BlockSpec((HEADS, tm, HEAD_DIM), lambda i: (0, i, 0)),
            pl.BlockSpec((tm, D_MODEL), lambda i: (i, 0)),
            pl.BlockSpec((CONV_K, WIDTH), lambda i: (0, 0)),
            pl.BlockSpec((2, WIDTH, D_MODEL), lambda i: (0, 0, 0), pipeline_mode=resident),
            pl.BlockSpec((D_MODEL, D_MODEL), lambda i: (0, 0), pipeline_mode=resident),
            pl.BlockSpec((1, D_MODEL), lambda i: (0, 0)),
        ],
        out_specs=pl.BlockSpec((tm, D_MODEL), lambda i: (i, 0)),
        out_shape=jax.ShapeDtypeStruct((tokens, D_MODEL), F32),
        compiler_params=pltpu.CompilerParams(
            dimension_semantics=("arbitrary",),
            vmem_limit_bytes=VMEM_LIMIT),
        name="merge",
    )(rows, rows, rows, ret, x2, conv_w, wb_bf, wo_bf, fgain)


def _rotary_tables(seq, tile_rows):
    inv_freq = ROPE_BASE ** (-jnp.arange(0, HEAD_DIM, 2, dtype=F32) / HEAD_DIM)
    inv_freq = jnp.concatenate([inv_freq, inv_freq])
    sign = jnp.where(jnp.arange(HEAD_DIM) < HEAD_DIM // 2, -1.0, 1.0).astype(F32)

    def table(pos):
        ang = pos.astype(F32)[:, None] * inv_freq[None, :]
        return jnp.stack([jnp.cos(ang), jnp.sin(ang), sign * jnp.sin(ang)])

    row_tab = table(jnp.arange(tile_rows))
    tile_tab = jnp.transpose(table(jnp.arange(seq // tile_rows) * tile_rows), (1, 0, 2))
    return row_tab, jnp.pad(tile_tab, ((0, 0), (0, 8 - 3), (0, 0)))


def kernel(x, norm_gain, w_in, decay_logit_fwd, decay_logit_bwd, ret_gn_gain, conv_w, w_branch,
           w_out, final_gain):
    batch, seq, d_model = x.shape
    assert d_model == D_MODEL and norm_gain.shape[0] == 1
    assert w_in.shape == (1, D_MODEL, N_COL_TILES * WIDTH)
    assert seq % PROJ_ROWS == 0 and seq % MERGE_ROWS == 0 and seq % CHUNK == 0
    tokens = batch * seq
    x2 = x.reshape(tokens, D_MODEL)
    row_tab, tile_tab = _rotary_tables(seq, PROJ_ROWS)
    gn_gain = ret_gn_gain[0].astype(F32)[None, :]
    heads, rows = _project(x2, norm_gain[0][None, :], gn_gain, w_in[0].astype(BF16), row_tab,
                           tile_tab, seq)

    logits = jnp.stack([decay_logit_fwd[0], decay_logit_bwd[0]], axis=1).astype(F32)
    logits = jnp.broadcast_to(logits[:, :, None, None], (HEADS, 2, 8, HEAD_DIM))
    ret = _retention(heads, logits, batch, seq)

    out = _merge(rows, ret, x2, conv_w[0], w_branch[0].astype(BF16), w_out[0].astype(BF16),
                 final_gain[None, :], seq)
    return out.reshape(batch, seq, D_MODEL)
```

```python
import functools

import jax
import jax.numpy as jnp
from jax import lax
from jax.experimental import pallas as pl
from jax.experimental.pallas import tpu as pltpu

F32 = jnp.float32
BF16 = jnp.bfloat16

D_MODEL = 2048
HEADS = 8
HEAD_DIM = 128
WIDTH = HEADS * HEAD_DIM
CHUNK = 128
CONV_K = 3
ROPE_BASE = 10000.0
EPS = 1e-6
G_Q, G_K, G_V, G_GRET, G_CB, G_CC, G_CX, G_GCONV, G_GATE0 = range(9)
N_COL_TILES = 12
N_HEAD_GROUPS = 4
R_U, R_CBG, R_GATE0 = 0, 1, 2
N_ROW_GROUPS = 6
S_CB, S_CC = 0, 1

PROJ_ROWS = 1024
PROJ_SUB_ROWS = 256
W_BUFFERS = 3
MERGE_ROWS = 512
MERGE_SUB_ROWS = 256
STATE_UNROLL = 64
OUT_UNROLL = 64
HALO_ROWS = 16
VMEM_LIMIT = 58 * 1024 * 1024


def _proj_kernel(x_hbm, gain_ref, gngain_ref, w_hbm, rowtab_ref, tiletab_ref, heads_ref, rows_ref,
                 x_ref, x_sem, w_buf, w_sem, h_ref, stash_ref):
    i = pl.program_id(0)
    j = pl.program_id(1)
    tm = h_ref.shape[0]

    def x_copy(tile):
        return pltpu.make_async_copy(x_hbm.at[pl.ds(tile * tm, tm), :], x_ref, x_sem)

    @pl.when(jnp.logical_and(i == 0, j == 0))
    def _():
        x_copy(0).start()

    @pl.when(jnp.logical_and(j == 1, i + 1 < pl.num_programs(0)))
    def _():
        x_copy(i + 1).start()

    @pl.when(j == 0)
    def _():
        x_copy(i).wait()

    n_col = pl.num_programs(1)
    step = i * n_col + j
    n_steps = pl.num_programs(0) * n_col

    def w_copy(at_step):
        slot = at_step % W_BUFFERS
        tile = at_step % n_col
        return pltpu.make_async_copy(w_hbm.at[:, pl.ds(tile * WIDTH, WIDTH)], w_buf.at[slot],
                                     w_sem.at[slot])

    @pl.when(step == 0)
    def _():
        for ahead in range(W_BUFFERS - 1):
            w_copy(ahead).start()

    @pl.when(step + W_BUFFERS - 1 < n_steps)
    def _():
        w_copy(step + W_BUFFERS - 1).start()

    w_copy(step).wait()
    w_ref = w_buf.at[step % W_BUFFERS]

    def row_blocks(normalize=False):
        for r0 in range(0, h_ref.shape[0], PROJ_SUB_ROWS):
            rs = slice(r0, r0 + PROJ_SUB_ROWS)
            if normalize:
                x = x_ref[rs, :]
                ms = jnp.mean(x * x, axis=-1, keepdims=True)
                h = (x * lax.rsqrt(ms + EPS) * gain_ref[...]).astype(BF16)
                h_ref[rs, :] = h
            else:
                h = h_ref[rs, :]
            yield rs, jnp.dot(h, w_ref[...], preferred_element_type=F32)

    def store_heads(rs, r):
        for hd in range(HEADS):
            heads_ref[hd, rs, :] = r[:, hd * HEAD_DIM:(hd + 1) * HEAD_DIM].astype(BF16)

    def store_rotary(rs, r, scale):
        tile_cos, tile_sin, tile_ssin = (tiletab_ref[0, t:t + 1, :] for t in range(3))
        cos = tile_cos * rowtab_ref[0, rs, :] - tile_sin * rowtab_ref[1, rs, :]
        sin = tile_ssin * rowtab_ref[0, rs, :] + tile_cos * rowtab_ref[2, rs, :]
        for hd in range(HEADS):
            rh = r[:, hd * HEAD_DIM:(hd + 1) * HEAD_DIM]
            o = rh * cos + pltpu.roll(rh, HEAD_DIM // 2, axis=1) * sin
            if scale is not None:
                o = o * scale
            heads_ref[hd, rs, :] = o.astype(BF16)

    @pl.when(j == G_Q)
    def _():
        for rs, r in row_blocks(normalize=True):
            store_rotary(rs, r, None)

    @pl.when(j == G_K)
    def _():
        for rs, r in row_blocks():
            store_rotary(rs, r, HEAD_DIM ** -0.5)

    @pl.when(j == G_V)
    def _():
        for rs, r in row_blocks():
            store_heads(rs, r)

    @pl.when(j == G_GRET)
    def _():
        for rs, r in row_blocks():
            store_heads(rs, r * jax.nn.sigmoid(r) * gngain_ref[...])

    @pl.when(jnp.logical_or(j == G_CB, j == G_CC))
    def _():
        for rs, r in row_blocks():
            stash_ref[j - G_CB, rs, :] = r.astype(BF16)

    @pl.when(j == G_CX)
    def _():
        for rs, r in row_blocks():
            rows_ref[0, rs, :] = (stash_ref[S_CC, rs, :].astype(F32) * r).astype(BF16)

    @pl.when(j == G_GCONV)
    def _():
        for rs, r in row_blocks():
            rows_ref[0, rs, :] = (stash_ref[S_CB, rs, :].astype(F32) * (r * jax.nn.sigmoid(r))
                                  ).astype(BF16)

    @pl.when(j >= G_GATE0)
    def _():
        for rs, r in row_blocks():
            rows_ref[0, rs, :] = jax.nn.sigmoid(r).astype(BF16)


def _project(x2, gain, gn_gain, w_bf, row_tab, tile_tab, seq):
    tokens = x2.shape[0]
    tm = PROJ_ROWS
    seq_tiles = seq // tm
    grid = (tokens // tm, N_COL_TILES)
    return pl.pallas_call(
        _proj_kernel,
        grid=grid,
        in_specs=[
            pl.BlockSpec(memory_space=pl.ANY),
            pl.BlockSpec((1, D_MODEL), lambda i, j: (0, 0)),
            pl.BlockSpec((1, WIDTH), lambda i, j: (0, 0)),
            pl.BlockSpec(memory_space=pl.ANY),
            pl.BlockSpec((3, tm, HEAD_DIM), lambda i, j: (0, 0, 0)),
            pl.BlockSpec((1, 8, HEAD_DIM), lambda i, j: (i % seq_tiles, 0, 0)),
        ],
        out_specs=[
            pl.BlockSpec((HEADS, tm, HEAD_DIM),
                         lambda i, j: (jnp.minimum(j, N_HEAD_GROUPS - 1), i, 0)),
            pl.BlockSpec((1, tm, WIDTH),
                         lambda i, j: (jnp.maximum(j - G_CX, 0), i, 0)),
        ],
        out_shape=[
            jax.ShapeDtypeStruct((N_HEAD_GROUPS * HEADS, tokens, HEAD_DIM), BF16),
            jax.ShapeDtypeStruct((N_ROW_GROUPS, tokens, WIDTH), BF16),
        ],
        scratch_shapes=[
            pltpu.VMEM((tm, D_MODEL), F32),
            pltpu.SemaphoreType.DMA(()),
            pltpu.VMEM((W_BUFFERS, D_MODEL, WIDTH), BF16),
            pltpu.SemaphoreType.DMA((W_BUFFERS,)),
            pltpu.VMEM((tm, D_MODEL), BF16),
            pltpu.VMEM((2, tm, WIDTH), BF16),
        ],
        compiler_params=pltpu.CompilerParams(
            dimension_semantics=("arbitrary", "arbitrary"),
            vmem_limit_bytes=VMEM_LIMIT),
        name="proj",
    )(x2, gain, gn_gain, w_bf, row_tab, tile_tab)


def _log_sigmoid(z):
    return -(jnp.maximum(-z, 0.0) + jnp.log1p(jnp.exp(-jnp.abs(z))))


def _ret_kernel(logit_ref, q_ref, k_ref, v_ref, g_ref, o_ref, sf_ref, sb_ref, p_ref, *, n_chunks):
    lf = _log_sigmoid(logit_ref[0, 0])[0:1, :]
    lb = _log_sigmoid(logit_ref[0, 1])[0:1, :]
    row = lax.broadcasted_iota(jnp.int32, (CHUNK, CHUNK), 0).astype(F32)
    col = lax.broadcasted_iota(jnp.int32, (CHUNK, CHUNK), 1).astype(F32)
    dist = row - col
    decay = jnp.exp(jnp.where(dist >= 0, dist * lf, -dist * lb))
    wq_f = jnp.exp(row * lf).astype(BF16)
    wq_b = jnp.exp((CHUNK - 1 - row) * lb).astype(BF16)
    wk_f = jnp.exp((CHUNK - row) * lf).astype(BF16)
    wk_b = jnp.exp((row + 1) * lb).astype(BF16)
    chunk_f = jnp.exp(CHUNK * lf)
    chunk_b = jnp.exp(CHUNK * lb)

    def rows_of(c):
        return pl.ds(pl.multiple_of(c * CHUNK, CHUNK), CHUNK)

    def kv_summary(k_weighted, v):
        return lax.dot_general(k_weighted, v, (((0,), (0,)), ((), ())),
                               preferred_element_type=F32)

    def state_body(i, carry):
        state_f, state_b = carry
        cf = i
        cb = n_chunks - 1 - i
        sf_ref[cf] = state_f.astype(BF16)
        sb_ref[cb] = state_b.astype(BF16)
        rows_f = rows_of(cf)
        rows_b = rows_of(cb)
        k_f = k_ref[0, rows_f, :]
        scores = lax.dot_general(q_ref[0, rows_f, :], k_f, (((1,), (1,)), ((), ())),
                                 preferred_element_type=F32)
        p_ref[rows_f, :] = (scores * decay).astype(BF16)
        return (chunk_f * state_f + kv_summary(k_f * wk_f, v_ref[0, rows_f, :]),
                chunk_b * state_b + kv_summary(k_ref[0, rows_b, :] * wk_b, v_ref[0, rows_b, :]))

    zero = jnp.zeros((CHUNK, HEAD_DIM), F32)
    lax.fori_loop(0, n_chunks, state_body, (zero, zero), unroll=STATE_UNROLL)

    def out_body(c, carry):
        rows = rows_of(c)
        q = q_ref[0, rows, :]
        lhs = jnp.concatenate([p_ref[rows, :], q * wq_f, q * wq_b], axis=1)
        rhs = jnp.concatenate([v_ref[0, rows, :], sf_ref[c], sb_ref[c]], axis=0)
        o = jnp.dot(lhs, rhs, preferred_element_type=F32)
        ms = jnp.mean(o * o, axis=-1, keepdims=True)
        o_ref[0, rows, :] = (o * lax.rsqrt(ms + EPS)).astype(BF16) * g_ref[0, rows, :]
        return carry

    lax.fori_loop(0, n_chunks, out_body, 0, unroll=OUT_UNROLL)


def _retention(heads, logits, batch, seq):
    tokens = batch * seq
    n_chunks = seq // CHUNK

    def head_spec(group):
        return pl.BlockSpec((1, seq, HEAD_DIM),
                            lambda p: (group * HEADS + p % HEADS, p // HEADS, 0))

    return pl.pallas_call(
        functools.partial(_ret_kernel, n_chunks=n_chunks),
        grid=(batch * HEADS,),
        in_specs=[
            pl.BlockSpec((1, 2, 8, HEAD_DIM), lambda p: (p % HEADS, 0, 0, 0)),
            head_spec(G_Q), head_spec(G_K), head_spec(G_V), head_spec(G_GRET),
        ],
        out_specs=pl.BlockSpec((1, seq, HEAD_DIM), lambda p: (p % HEADS, p // HEADS, 0)),
        out_shape=jax.ShapeDtypeStruct((HEADS, tokens, HEAD_DIM), BF16),
        scratch_shapes=[pltpu.VMEM((n_chunks, CHUNK, HEAD_DIM), BF16),
                        pltpu.VMEM((n_chunks, CHUNK, HEAD_DIM), BF16),
                        pltpu.VMEM((seq, CHUNK), BF16)],
        compiler_params=pltpu.CompilerParams(
            dimension_semantics=("arbitrary",),
            vmem_limit_bytes=VMEM_LIMIT),
        name="retention",
    )(logits, heads, heads, heads, heads)


def _merge_kernel(rows_ref, uprev_ref, unext_ref, ret_ref, x_ref, convw_ref, wb_ref, wo_ref,
                  fgain_ref, out_ref, *, seq_tiles):
    tm = x_ref.shape[0]
    i = pl.program_id(0)
    pos = i % seq_tiles
    u = rows_ref[R_U].astype(F32)
    prev_row = jnp.where(pos == 0, 0.0, uprev_ref[0, HALO_ROWS - 1:HALO_ROWS, :].astype(F32))
    next_row = jnp.where(pos == seq_tiles - 1, 0.0, unext_ref[0, 0:1, :].astype(F32))
    ridx = lax.broadcasted_iota(jnp.int32, (tm, WIDTH), 0)
    u_prev = jnp.where(ridx == 0, prev_row, pltpu.roll(u, 1, axis=0))
    u_next = jnp.where(ridx == tm - 1, next_row, pltpu.roll(u, tm - 1, axis=0))
    w = convw_ref[...]
    conv = w[0:1, :] * u_prev + w[1:2, :] * u + w[2:3, :] * u_next
    branch_conv = (rows_ref[R_CBG].astype(F32) * conv).astype(BF16)

    def up_project(rs):
        branch_ret = jnp.concatenate([ret_ref[hd, rs, :] for hd in range(HEADS)], axis=1)
        ups = []
        for half in range(D_MODEL // WIDTH):
            cols = slice(half * WIDTH, (half + 1) * WIDTH)
            ups.append((jnp.dot(branch_ret, wb_ref[0, :, cols], preferred_element_type=F32),
                        jnp.dot(branch_conv[rs, :], wb_ref[1, :, cols],
                                preferred_element_type=F32)))
        return ups

    def gated_merge(rs, ups):
        halves = []
        for half, (up_ret, up_conv) in enumerate(ups):
            gate_ret = rows_ref[R_GATE0 + half, rs, :].astype(F32)
            gate_conv = rows_ref[R_GATE0 + D_MODEL // WIDTH + half, rs, :].astype(F32)
            halves.append((gate_ret * up_ret + gate_conv * up_conv).astype(BF16))
        return jnp.concatenate(halves, axis=1)

    def out_project(rs, merged):
        return x_ref[rs, :] + jnp.dot(merged, wo_ref[...], preferred_element_type=F32)

    def normalise(rs, y):
        ms = jnp.mean(y * y, axis=-1, keepdims=True)
        out_ref[rs, :] = y * lax.rsqrt(ms + EPS) * fgain_ref[...]

    blocks = [slice(r0, r0 + MERGE_SUB_ROWS) for r0 in range(0, tm, MERGE_SUB_ROWS)]
    ups = [up_project(rs) for rs in blocks]
    outs = []
    for rs, up in zip(blocks, ups):
        outs.append(out_project(rs, gated_merge(rs, up)))
    for rs, y in zip(blocks, outs):
        normalise(rs, y)


def _merge(rows, ret, x2, conv_w, wb_bf, wo_bf, fgain, seq):
    tokens = x2.shape[0]
    tm = MERGE_ROWS
    seq_tiles = seq // tm
    halo_per_tile = tm // HALO_ROWS
    n_halo = tokens // HALO_ROWS
    resident = pl.Buffered(1)
    return pl.pallas_call(
        functools.partial(_merge_kernel, seq_tiles=seq_tiles),
        grid=(tokens // tm,),
        in_specs=[
            pl.BlockSpec((N_ROW_GROUPS, tm, WIDTH), lambda i: (0, i, 0)),
            pl.BlockSpec((1, HALO_ROWS, WIDTH),
                         lambda i: (R_U, jnp.maximum(i * halo_per_tile - 1, 0), 0)),
            pl.BlockSpec((1, HALO_ROWS, WIDTH),
                         lambda i: (R_U, jnp.minimum((i + 1) * halo_per_tile, n_halo - 1), 0)),
            pl.BlockSpec((HEADS, tm, HEAD_DIM), lambda i: (0, i, 0)),
            pl.BlockSpec((tm, D_MODEL), lambda i: (i, 0)),
            pl.BlockSpec((CONV_K, WIDTH), lambda i: (0, 0)),
            pl.BlockSpec((2, WIDTH, D_MODEL), lambda i: (0, 0, 0), pipeline_mode=resident),
            pl.BlockSpec((D_MODEL, D_MODEL), lambda i: (0, 0), pipeline_mode=resident),
            pl.BlockSpec((1, D_MODEL), lambda i: (0, 0)),
        ],
        out_specs=pl.BlockSpec((tm, D_MODEL), lambda i: (i, 0)),
        out_shape=jax.ShapeDtypeStruct((tokens, D_MODEL), F32),
        compiler_params=pltpu.CompilerParams(
            dimension_semantics=("arbitrary",),
            vmem_limit_bytes=VMEM_LIMIT),
        name="merge",
    )(rows, rows, rows, ret, x2, conv_w, wb_bf, wo_bf, fgain)


def _rotary_tables(seq, tile_rows):
    inv_freq = ROPE_BASE ** (-jnp.arange(0, HEAD_DIM, 2, dtype=F32) / HEAD_DIM)
    inv_freq = jnp.concatenate([inv_freq, inv_freq])
    sign = jnp.where(jnp.arange(HEAD_DIM) < HEAD_DIM // 2, -1.0, 1.0).astype(F32)

    def table(pos):
        ang = pos.astype(F32)[:, None] * inv_freq[None, :]
        return jnp.stack([jnp.cos(ang), jnp.sin(ang), sign * jnp.sin(ang)])

    row_tab = table(jnp.arange(tile_rows))
    tile_tab = jnp.transpose(table(jnp.arange(seq // tile_rows) * tile_rows), (1, 0, 2))
    return row_tab, jnp.pad(tile_tab, ((0, 0), (0, 8 - 3), (0, 0)))


def kernel(x, norm_gain, w_in, decay_logit_fwd, decay_logit_bwd, ret_gn_gain, conv_w, w_branch,
           w_out, final_gain):
    batch, seq, d_model = x.shape
    assert d_model == D_MODEL and norm_gain.shape[0] == 1
    assert w_in.shape == (1, D_MODEL, N_COL_TILES * WIDTH)
    assert seq % PROJ_ROWS == 0 and seq % MERGE_ROWS == 0 and seq % CHUNK == 0
    tokens = batch * seq
    x2 = x.reshape(tokens, D_MODEL)
    row_tab, tile_tab = _rotary_tables(seq, PROJ_ROWS)
    gn_gain = ret_gn_gain[0].astype(F32)[None, :]
    heads, rows = _project(x2, norm_gain[0][None, :], gn_gain, w_in[0].astype(BF16), row_tab,
                           tile_tab, seq)

    logits = jnp.stack([decay_logit_fwd[0], decay_logit_bwd[0]], axis=1).astype(F32)
    logits = jnp.broadcast_to(logits[:, :, None, None], (HEADS, 2, 8, HEAD_DIM))
    ret = _retention(heads, logits, batch, seq)

    out = _merge(rows, ret, x2, conv_w[0], w_branch[0].astype(BF16), w_out[0].astype(BF16),
                 final_gain[None, :], seq)
    return out.reshape(batch, seq, D_MODEL)
```
